```python
import math
import jax, jax.numpy as jnp
from jax import lax
import numpy as np

D_MODEL = 2048
BATCH = 2
SEQ = 4096
DEPTH = 4

HEAD_DIM = 128
N_SB_HEADS = 8
N_DIFF_HEADS = 4
SB_WIDTH = N_SB_HEADS * HEAD_DIM
DIFF_QK_WIDTH = N_DIFF_HEADS * 2 * HEAD_DIM
DIFF_V_WIDTH = N_DIFF_HEADS * 2 * HEAD_DIM
D_MIX = SB_WIDTH + DIFF_V_WIDTH
D_IN = 3 * SB_WIDTH + 2 * DIFF_QK_WIDTH + DIFF_V_WIDTH
D_FF = 5632
Q_BLOCK = 128
ROPE_THETA = 10000.0
RMS_EPS = 1e-6
FFN_RESIDUAL_WEIGHT = 0.5

kernel_name = "hymba_stickbreak_diffattn_macaron"


def rmsnorm(x, g):
    xf = x.astype(jnp.float32)
    y = xf * lax.rsqrt(jnp.mean(xf * xf, axis=-1, keepdims=True) + RMS_EPS)
    return (y * g.astype(jnp.float32)).astype(x.dtype)


def swiglu(x, w_gate, w_up, w_down):
    return (jax.nn.silu(x @ w_gate) * (x @ w_up)) @ w_down


def rope_tables(seq):
    pos = jnp.arange(seq, dtype=jnp.float32)
    inv_freq = ROPE_THETA ** (-jnp.arange(0, HEAD_DIM, 2, dtype=jnp.float32) / HEAD_DIM)
    ang = pos[:, None] * inv_freq[None, :]
    ang = jnp.concatenate([ang, ang], axis=-1)
    return jnp.cos(ang), jnp.sin(ang)


def apply_rope(x, cos, sin):
    xf = x.astype(jnp.float32)
    half = HEAD_DIM // 2
    rot = jnp.concatenate([-xf[..., half:], xf[..., :half]], axis=-1)
    return (xf * cos + rot * sin).astype(x.dtype)


def to_blocks(q):
    b, h, s, d = q.shape
    return q.reshape(b, h, s // Q_BLOCK, Q_BLOCK, d).transpose(2, 0, 1, 3, 4)


def from_blocks(o):
    nb, b, h, qb, d = o.shape
    return o.transpose(1, 0, 3, 2, 4).reshape(b, nb * qb, h * d)


def stick_breaking_attention(q, k, v):
    seq = q.shape[2]
    scale = 1.0 / math.sqrt(HEAD_DIM)
    key_pos = jnp.arange(seq)

    def block(args):
        qb, i = args
        q_pos = i * Q_BLOCK + jnp.arange(Q_BLOCK)
        mask = key_pos[None, :] < q_pos[:, None]
        z = jnp.einsum('bhqd,bhkd->bhqk', qb, k).astype(jnp.float32) * scale
        log_beta = jax.nn.log_sigmoid(z)
        log_one_minus = jnp.where(mask, jax.nn.log_sigmoid(-z), 0.0)
        key_axis = log_one_minus.ndim - 1
        tail = lax.cumsum(log_one_minus, axis=key_axis, reverse=True) - log_one_minus
        w = jnp.where(mask, jnp.exp(log_beta + tail), 0.0)
        return jnp.einsum('bhqk,bhkd->bhqd', w.astype(v.dtype), v)

    nb = seq // Q_BLOCK
    out = lax.map(block, (to_blocks(q), jnp.arange(nb)))
    return from_blocks(out)


def differential_attention(q1, q2, k1, k2, v, lam):
    seq = q1.shape[2]
    scale = 1.0 / math.sqrt(HEAD_DIM)
    key_pos = jnp.arange(seq)

    def block(args):
        q1b, q2b, i = args
        q_pos = i * Q_BLOCK + jnp.arange(Q_BLOCK)
        mask = key_pos[None, :] <= q_pos[:, None]

        def probs(qb, kk):
            z = jnp.einsum('bhqd,bhkd->bhqk', qb, kk).astype(jnp.float32) * scale
            z = jnp.where(mask, z, -jnp.inf)
            return jax.nn.softmax(z, axis=-1)

        a = probs(q1b, k1) - lam * probs(q2b, k2)
        return jnp.einsum('bhqk,bhkd->bhqd', a.astype(v.dtype), v)

    nb = seq // Q_BLOCK
    out = lax.map(block, (to_blocks(q1), to_blocks(q2), jnp.arange(nb)))
    return out


def mixing_sublayer(h, w_in, lambda_q1, lambda_k1, lambda_q2, lambda_k2, subln, w_out,
                    cos, sin, lambda_init):
    b, s, _ = h.shape
    proj = h @ w_in
    splits = [int(c) for c in np.cumsum([SB_WIDTH, SB_WIDTH, SB_WIDTH, DIFF_QK_WIDTH, DIFF_QK_WIDTH])]
    sb_q, sb_k, sb_v, d_q, d_k, d_v = jnp.split(proj, splits, axis=-1)

    def heads(t, n, d):
        return t.reshape(b, s, n, d).transpose(0, 2, 1, 3)

    sb_out = stick_breaking_attention(heads(sb_q, N_SB_HEADS, HEAD_DIM),
                                      heads(sb_k, N_SB_HEADS, HEAD_DIM),
                                      heads(sb_v, N_SB_HEADS, HEAD_DIM))

    dq = d_q.reshape(b, s, N_DIFF_HEADS, 2, HEAD_DIM).transpose(3, 0, 2, 1, 4)
    dk = d_k.reshape(b, s, N_DIFF_HEADS, 2, HEAD_DIM).transpose(3, 0, 2, 1, 4)
    dv = heads(d_v, N_DIFF_HEADS, 2 * HEAD_DIM)
    q1, q2 = apply_rope(dq[0], cos, sin), apply_rope(dq[1], cos, sin)
    k1, k2 = apply_rope(dk[0], cos, sin), apply_rope(dk[1], cos, sin)
    lam = (jnp.exp(jnp.sum(lambda_q1.astype(jnp.float32) * lambda_k1.astype(jnp.float32)))
           - jnp.exp(jnp.sum(lambda_q2.astype(jnp.float32) * lambda_k2.astype(jnp.float32)))
           + lambda_init)
    d_heads = differential_attention(q1, q2, k1, k2, dv, lam)
    d_heads = rmsnorm(d_heads, subln) * (1.0 - lambda_init)
    diff_out = from_blocks(d_heads)

    mixed = jnp.concatenate([sb_out, diff_out], axis=-1)
    return mixed @ w_out


def setup_inputs(seed: int = 0) -> dict:
    key = jax.random.key(seed)
    ks = jax.random.split(key, 20)
    f32 = jnp.float32

    def normal(k, shape, scale):
        return jax.random.normal(k, shape, f32) * scale

    def gain(k, shape):
        return 1.0 + 0.01 * jax.random.normal(k, shape, f32)

    return {
        "x": normal(ks[0], (BATCH, SEQ, D_MODEL), 1.0),
        "norm_ffn1": gain(ks[1], (DEPTH, D_MODEL)),
        "w_ffn1_gate": normal(ks[2], (DEPTH, D_MODEL, D_FF), D_MODEL ** -0.5),
        "w_ffn1_up": normal(ks[3], (DEPTH, D_MODEL, D_FF), D_MODEL ** -0.5),
        "w_ffn1_down": normal(ks[4], (DEPTH, D_FF, D_MODEL), D_FF ** -0.5),
        "norm_mix": gain(ks[5], (DEPTH, D_MODEL)),
        "w_in": normal(ks[6], (DEPTH, D_MODEL, D_IN), D_MODEL ** -0.5),
        "lambda_q1": normal(ks[7], (DEPTH, HEAD_DIM), 0.1),
        "lambda_k1": normal(ks[8], (DEPTH, HEAD_DIM), 0.1),
        "lambda_q2": normal(ks[9], (DEPTH, HEAD_DIM), 0.1),
        "lambda_k2": normal(ks[10], (DEPTH, HEAD_DIM), 0.1),
        "diff_subln": gain(ks[11], (DEPTH, 2 * HEAD_DIM)),
        "w_out": normal(ks[12], (DEPTH, D_MIX, D_MODEL), D_MIX ** -0.5),
        "norm_ffn2": gain(ks[13], (DEPTH, D_MODEL)),
        "w_ffn2_gate": normal(ks[14], (DEPTH, D_MODEL, D_FF), D_MODEL ** -0.5),
        "w_ffn2_up": normal(ks[15], (DEPTH, D_MODEL, D_FF), D_MODEL ** -0.5),
        "w_ffn2_down": normal(ks[16], (DEPTH, D_FF, D_MODEL), D_FF ** -0.5),
        "norm_final": gain(ks[17], (D_MODEL,)),
    }


def reference(x, norm_ffn1, w_ffn1_gate, w_ffn1_up, w_ffn1_down, norm_mix, w_in,
              lambda_q1, lambda_k1, lambda_q2, lambda_k2, diff_subln, w_out,
              norm_ffn2, w_ffn2_gate, w_ffn2_up, w_ffn2_down, norm_final):
    cos, sin = rope_tables(x.shape[1])
    for layer in range(DEPTH):
        lambda_init = 0.8 - 0.6 * math.exp(-0.3 * layer)
        h = rmsnorm(x, norm_ffn1[layer])
        x = x + FFN_RESIDUAL_WEIGHT * swiglu(h, w_ffn1_gate[layer], w_ffn1_up[layer], w_ffn1_down[layer])
        h = rmsnorm(x, norm_mix[layer])
        x = x + mixing_sublayer(h, w_in[layer], lambda_q1[layer], lambda_k1[layer],
                                lambda_q2[layer], lambda_k2[layer], diff_subln[layer],
                                w_out[layer], cos, sin, lambda_init)
        h = rmsnorm(x, norm_ffn2[layer])
        x = x + FFN_RESIDUAL_WEIGHT * swiglu(h, w_ffn2_gate[layer], w_ffn2_up[layer], w_ffn2_down[layer])
    return rmsnorm(x, norm_final)
```

```python
import functools
import math

import jax
import jax.numpy as jnp
from jax import lax
from jax.experimental import pallas as pl
from jax.experimental.pallas import tpu as pltpu

HEAD_DIM = 128
N_SB_HEADS = 8
N_DIFF_HEADS = 4
SB_WIDTH = N_SB_HEADS * HEAD_DIM
DIFF_WIDTH = N_DIFF_HEADS * 2 * HEAD_DIM
ROPE_THETA = 10000.0
RMS_EPS = 1e-6
FFN_RESIDUAL_WEIGHT = 0.5
QK_SCALE = 1.0 / math.sqrt(HEAD_DIM)

V7X_VMEM_BYTES = 64 * 1024 * 1024
MIB = 1024 * 1024
NEG_BIG = -1e30

F32 = jnp.float32
BF16 = jnp.bfloat16


def _rmsnorm_rows(x, g):
    ms = jnp.mean(x * x, axis=-1, keepdims=True)
    return x * lax.rsqrt(ms + RMS_EPS) * g


def _dot(a, b):
    return jnp.dot(a, b, preferred_element_type=F32)


def _dot_nt(a, b):
    return lax.dot_general(a, b, (((1,), (1,)), ((), ())), preferred_element_type=F32)


def _ffn_kernel(x_ref, g_ref, wg_ref, wu_ref, wd_ref, gf_ref, o_ref, h_ref, *, final_norm):
    j = pl.program_id(1)
    nj = pl.num_programs(1)

    @pl.when(j == 0)
    def _():
        x = x_ref[...]
        h_ref[...] = _rmsnorm_rows(x, g_ref[...]).astype(BF16)
        o_ref[...] = x

    h = h_ref[...]
    gate = _dot(h, wg_ref[...])
    up = _dot(h, wu_ref[...])
    act = (gate / (1.0 + jnp.exp(-gate))) * (up * FFN_RESIDUAL_WEIGHT)
    o_ref[...] += _dot(act.astype(BF16), wd_ref[...])

    if final_norm:
        @pl.when(j == nj - 1)
        def _():
            o_ref[...] = _rmsnorm_rows(o_ref[...], gf_ref[...])


def _ffn(x, g, wg, wu, wd, g_final, *, final_norm, tm=1024, tf=512):
    t, d = x.shape
    d_ff = wg.shape[1]
    assert t % tm == 0 and d_ff % tf == 0
    return pl.pallas_call(
        functools.partial(_ffn_kernel, final_norm=final_norm),
        out_shape=jax.ShapeDtypeStruct((t, d), F32),
        grid=(t // tm, d_ff // tf),
        in_specs=[
            pl.BlockSpec((tm, d), lambda i, j: (i, 0), pipeline_mode=pl.Buffered(1)),
            pl.BlockSpec((1, d), lambda i, j: (0, 0)),
            pl.BlockSpec((d, tf), lambda i, j: (0, j)),
            pl.BlockSpec((d, tf), lambda i, j: (0, j)),
            pl.BlockSpec((tf, d), lambda i, j: (j, 0)),
            pl.BlockSpec((1, d), lambda i, j: (0, 0)),
        ],
        out_specs=pl.BlockSpec((tm, d), lambda i, j: (i, 0)),
        scratch_shapes=[pltpu.VMEM((tm, d), BF16)],
        compiler_params=pltpu.CompilerParams(
            dimension_semantics=("parallel", "arbitrary"),
            vmem_limit_bytes=56 * MIB,
        ),
        name="ffn_swiglu",
    )(x, g, wg, wu, wd, g_final)


def _inproj_kernel(x_ref, g_ref, w_ref, cos_ref, sin_ref, o_ref, h_ref, *, tn):
    j = pl.program_id(1)

    @pl.when(j == 0)
    def _():
        h_ref[...] = _rmsnorm_rows(x_ref[...], g_ref[...]).astype(BF16)

    acc = _dot(h_ref[...], w_ref[...])
    col = j * tn
    sbq_end = SB_WIDTH
    dq_start = 3 * SB_WIDTH
    dk_start = dq_start + DIFF_WIDTH
    dv_start = dk_start + DIFF_WIDTH

    def rope(y):
        cos = cos_ref[...]
        sin = sin_ref[...]
        parts = []
        for c in range(tn // HEAD_DIM):
            yc = y[:, c * HEAD_DIM:(c + 1) * HEAD_DIM]
            parts.append(yc * cos + pltpu.roll(yc, HEAD_DIM // 2, 1) * sin)
        return jnp.concatenate(parts, axis=1)

    @pl.when(col < sbq_end)
    def _():
        o_ref[...] = (acc * QK_SCALE).astype(o_ref.dtype)

    @pl.when(jnp.logical_or(jnp.logical_and(col >= sbq_end, col < dq_start), col >= dv_start))
    def _():
        o_ref[...] = acc.astype(o_ref.dtype)

    @pl.when(jnp.logical_and(col >= dq_start, col < dk_start))
    def _():
        o_ref[...] = (rope(acc) * QK_SCALE).astype(o_ref.dtype)

    @pl.when(jnp.logical_and(col >= dk_start, col < dv_start))
    def _():
        o_ref[...] = rope(acc).astype(o_ref.dtype)


def _inproj(x, g, w, cos, sin, seq, *, tm=1024, tn=512):
    t, d = x.shape
    d_in = w.shape[1]
    assert t % tm == 0 and d_in % tn == 0 and seq % tm == 0 and SB_WIDTH % tn == 0
    pos_blocks = seq // tm
    return pl.pallas_call(
        functools.partial(_inproj_kernel, tn=tn),
        out_shape=jax.ShapeDtypeStruct((t, d_in), BF16),
        grid=(t // tm, d_in // tn),
        in_specs=[
            pl.BlockSpec((tm, d), lambda i, j: (i, 0)),
            pl.BlockSpec((1, d), lambda i, j: (0, 0)),
            pl.BlockSpec((d, tn), lambda i, j: (0, j)),
            pl.BlockSpec((tm, HEAD_DIM), lambda i, j: (i % pos_blocks, 0)),
            pl.BlockSpec((tm, HEAD_DIM), lambda i, j: (i % pos_blocks, 0)),
        ],
        out_specs=pl.BlockSpec((tm, tn), lambda i, j: (i, j)),
        scratch_shapes=[pltpu.VMEM((tm, d), BF16)],
        compiler_params=pltpu.CompilerParams(
            dimension_semantics=("parallel", "arbitrary"),
            vmem_limit_bytes=48 * MIB,
        ),
        name="inproj_rope",
    )(x, g, w, cos, sin)


def _sb_tile(q, k, v, tri, carry, mask):
    z = _dot_nt(q, k)
    lom = jnp.minimum(-z, 0.0) - jnp.log(1.0 + jnp.exp(-jnp.abs(z)))
    if mask is not None:
        lom = jnp.where(mask, lom, 0.0)
    hi = lom.astype(BF16)
    lo = (lom - hi.astype(F32)).astype(BF16)
    incl = _dot(hi, tri) + _dot(lo, tri)
    w = jnp.exp(z + incl + carry)
    if mask is not None:
        w = jnp.where(mask, w, 0.0)
    out = _dot(w.astype(BF16), v)
    return out, carry + incl[:, 0:1]


def _sb_kernel(q_ref, k_ref, v_ref, tri_ref, o_ref, acc_ref, carry_ref, *, tq):
    i = pl.program_id(2)
    q = q_ref[0]
    tri = tri_ref[...]

    start = pl.multiple_of(i * tq, tq)
    row = lax.broadcasted_iota(jnp.int32, (tq, tq), 0)
    colm = lax.broadcasted_iota(jnp.int32, (tq, tq), 1)
    out, carry = _sb_tile(q, k_ref[0, pl.ds(start, tq), :], v_ref[0, pl.ds(start, tq), :], tri,
                          jnp.zeros((tq, 1), F32), colm < row)
    acc_ref[...] = out
    carry_ref[...] = carry

    def body(step, _):
        kb = i - 1 - step
        s0 = pl.multiple_of(kb * tq, tq)
        out, carry = _sb_tile(q, k_ref[0, pl.ds(s0, tq), :], v_ref[0, pl.ds(s0, tq), :], tri,
                              carry_ref[...], None)
        acc_ref[...] += out
        carry_ref[...] = carry
        return 0

    lax.fori_loop(0, i, body, 0)
    o_ref[0] = acc_ref[...].astype(o_ref.dtype)


def _sb_attention(proj, tri, *, tq=256):
    b, s, _ = proj.shape
    assert s % tq == 0
    qoff, koff, voff = 0, N_SB_HEADS, 2 * N_SB_HEADS
    return pl.pallas_call(
        functools.partial(_sb_kernel, tq=tq),
        out_shape=jax.ShapeDtypeStruct((b, s, SB_WIDTH), BF16),
        grid=(b, N_SB_HEADS, s // tq),
        in_specs=[
            pl.BlockSpec((1, tq, HEAD_DIM), lambda bi, h, i: (bi, i, qoff + h)),
            pl.BlockSpec((1, s, HEAD_DIM), lambda bi, h, i: (bi, 0, koff + h)),
            pl.BlockSpec((1, s, HEAD_DIM), lambda bi, h, i: (bi, 0, voff + h)),
            pl.BlockSpec((tq, tq), lambda bi, h, i: (0, 0)),
        ],
        out_specs=pl.BlockSpec((1, tq, HEAD_DIM), lambda bi, h, i: (bi, i, h)),
        scratch_shapes=[pltpu.VMEM((tq, HEAD_DIM), F32), pltpu.VMEM((tq, 1), F32)],
        compiler_params=pltpu.CompilerParams(
            dimension_semantics=("parallel", "parallel", "arbitrary"),
            vmem_limit_bytes=32 * MIB,
        ),
        name="stickbreak_attn",
    )(proj, proj, proj, tri)


def _diff_update(z, v, m_ref, l_ref, acc_ref):
    m_old = m_ref[...]
    m_new = jnp.maximum(m_old, jnp.max(z, axis=-1, keepdims=True))
    alpha = jnp.exp(m_old - m_new)
    p = jnp.exp(z - m_new)
    l_ref[...] = alpha * l_ref[...] + jnp.sum(p, axis=-1, keepdims=True)
    acc_ref[...] = alpha * acc_ref[...] + _dot(p.astype(BF16), v)
    m_ref[...] = m_new


def _diff_kernel(q_ref, k_ref, v_ref, lq1_ref, lk1_ref, lq2_ref, lk2_ref, sub_ref, o_ref,
                 m1_ref, l1_ref, a1_ref, m2_ref, l2_ref, a2_ref, *, tq, lambda_init):
    i = pl.program_id(2)
    q = q_ref[0]
    q1 = q[:, :HEAD_DIM]
    q2 = q[:, HEAD_DIM:]

    m1_ref[...] = jnp.full_like(m1_ref, NEG_BIG)
    m2_ref[...] = jnp.full_like(m2_ref, NEG_BIG)
    l1_ref[...] = jnp.zeros_like(l1_ref)
    l2_ref[...] = jnp.zeros_like(l2_ref)
    a1_ref[...] = jnp.zeros_like(a1_ref)
    a2_ref[...] = jnp.zeros_like(a2_ref)

    def tile(s0, mask):
        k = k_ref[0, pl.ds(s0, tq), :]
        v = v_ref[0, pl.ds(s0, tq), :]
        z1 = _dot_nt(q1, k[:, :HEAD_DIM])
        z2 = _dot_nt(q2, k[:, HEAD_DIM:])
        if mask is not None:
            z1 = jnp.where(mask, z1, NEG_BIG)
            z2 = jnp.where(mask, z2, NEG_BIG)
        _diff_update(z1, v, m1_ref, l1_ref, a1_ref)
        _diff_update(z2, v, m2_ref, l2_ref, a2_ref)

    row = lax.broadcasted_iota(jnp.int32, (tq, tq), 0)
    colm = lax.broadcasted_iota(jnp.int32, (tq, tq), 1)
    tile(pl.multiple_of(i * tq, tq), colm <= row)

    def body(step, _):
        tile(pl.multiple_of(step * tq, tq), None)
        return 0

    lax.fori_loop(0, i, body, 0)

    lam = (jnp.exp(jnp.sum(lq1_ref[...] * lk1_ref[...])) - jnp.exp(jnp.sum(lq2_ref[...] * lk2_ref[...]))
           + lambda_init)
    o = a1_ref[...] / l1_ref[...] - lam * (a2_ref[...] / l2_ref[...])
    o = _rmsnorm_rows(o, sub_ref[...]) * (1.0 - lambda_init)
    o_ref[0] = o.astype(o_ref.dtype)


def _diff_attention(proj, lq1, lk1, lq2, lk2, subln, lambda_init, *, tq=256):
    b, s, _ = proj.shape
    assert s % tq == 0
    dv = 2 * HEAD_DIM
    qoff = 3 * SB_WIDTH // dv
    koff = qoff + N_DIFF_HEADS
    voff = koff + N_DIFF_HEADS
    vec = pl.BlockSpec((1, HEAD_DIM), lambda bi, h, i: (0, 0))
    return pl.pallas_call(
        functools.partial(_diff_kernel, tq=tq, lambda_init=lambda_init),
        out_shape=jax.ShapeDtypeStruct((b, s, DIFF_WIDTH), BF16),
        grid=(b, N_DIFF_HEADS, s // tq),
        in_specs=[
            pl.BlockSpec((1, tq, dv), lambda bi, h, i: (bi, i, qoff + h)),
            pl.BlockSpec((1, s, dv), lambda bi, h, i: (bi, 0, koff + h)),
            pl.BlockSpec((1, s, dv), lambda bi, h, i: (bi, 0, voff + h)),
            vec, vec, vec, vec,
            pl.BlockSpec((1, dv), lambda bi, h, i: (0, 0)),
        ],
        out_specs=pl.BlockSpec((1, tq, dv), lambda bi, h, i: (bi, i, h)),
        scratch_shapes=[
            pltpu.VMEM((tq, 1), F32), pltpu.VMEM((tq, 1), F32), pltpu.VMEM((tq, dv), F32),
            pltpu.VMEM((tq, 1), F32), pltpu.VMEM((tq, 1), F32), pltpu.VMEM((tq, dv), F32),
        ],
        compiler_params=pltpu.CompilerParams(
            dimension_semantics=("parallel", "parallel", "arbitrary"),
            vmem_limit_bytes=32 * MIB,
        ),
        name="diff_attn",
    )(proj, proj, proj, lq1, lk1, lq2, lk2, subln)


def _outproj_kernel(x_ref, sb_ref, df_ref, w1_ref, w2_ref, o_ref):
    o_ref[...] = x_ref[...] + _dot(sb_ref[...], w1_ref[...]) + _dot(df_ref[...], w2_ref[...])


def _outproj(x, sb, df, w_out, *, tm=1024, tn=1024):
    t, d = x.shape
    assert t % tm == 0 and d % tn == 0
    k1 = sb.shape[1]
    k2 = df.shape[1]
    assert k1 % tn == 0 or tn % k1 == 0
    return pl.pallas_call(
        _outproj_kernel,
        out_shape=jax.ShapeDtypeStruct((t, d), F32),
        grid=(t // tm, d // tn),
        in_specs=[
            pl.BlockSpec((tm, tn), lambda i, j: (i, j)),
            pl.BlockSpec((tm, k1), lambda i, j: (i, 0)),
            pl.BlockSpec((tm, k2), lambda i, j: (i, 0)),
            pl.BlockSpec((k1, tn), lambda i, j: (0, j)),
            pl.BlockSpec((k2, tn), lambda i, j: (1, j)),
        ],
        out_specs=pl.BlockSpec((tm, tn), lambda i, j: (i, j)),
        compiler_params=pltpu.CompilerParams(
            dimension_semantics=("parallel", "arbitrary"),
            vmem_limit_bytes=48 * MIB,
        ),
        name="outproj_residual",
    )(x, sb, df, w_out, w_out)


def _rope_tables(seq):
    pos = jnp.arange(seq, dtype=F32)
    inv_freq = ROPE_THETA ** (-jnp.arange(0, HEAD_DIM, 2, dtype=F32) / HEAD_DIM)
    ang = pos[:, None] * inv_freq[None, :]
    ang = jnp.concatenate([ang, ang], axis=-1)
    sign = jnp.concatenate([-jnp.ones((HEAD_DIM // 2,), F32), jnp.ones((HEAD_DIM // 2,), F32)])
    return jnp.cos(ang), jnp.sin(ang) * sign[None, :]


def kernel(x, norm_ffn1, w_ffn1_gate, w_ffn1_up, w_ffn1_down, norm_mix, w_in, lambda_q1, lambda_k1, lambda_q2, lambda_k2, diff_subln, w_out, norm_ffn2, w_ffn2_gate, w_ffn2_up, w_ffn2_down, norm_final):
    b, s, d = x.shape
    depth = w_in.shape[0]
    cos, sin = _rope_tables(s)
    tq = 256
    tri = (lax.broadcasted_iota(jnp.int32, (tq, tq), 0) >= lax.broadcasted_iota(jnp.int32, (tq, tq), 1)).astype(BF16)
    g_final = norm_final.reshape(1, d)

    xt = x.reshape(b * s, d)
    for layer in range(depth):
        lambda_init = 0.8 - 0.6 * math.exp(-0.3 * layer)
        xt = _ffn(xt, norm_ffn1[layer].reshape(1, d), w_ffn1_gate[layer].astype(BF16),
                  w_ffn1_up[layer].astype(BF16), w_ffn1_down[layer].astype(BF16), g_final, final_norm=False)
        proj = _inproj(xt, norm_mix[layer].reshape(1, d), w_in[layer].astype(BF16), cos, sin, s)
        proj = proj.reshape(b, s, -1)
        sb = _sb_attention(proj, tri, tq=tq)
        df = _diff_attention(proj, lambda_q1[layer].reshape(1, -1), lambda_k1[layer].reshape(1, -1),
                             lambda_q2[layer].reshape(1, -1), lambda_k2[layer].reshape(1, -1),
                             diff_subln[layer].reshape(1, -1), lambda_init, tq=tq)
        xt = _outproj(xt, sb.reshape(b * s, -1), df.reshape(b * s, -1), w_out[layer].astype(BF16))
        xt = _ffn(xt, norm_ffn2[layer].reshape(1, d), w_ffn2_gate[layer].astype(BF16),
                  w_ffn2_up[layer].astype(BF16), w_ffn2_down[layer].astype(BF16), g_final,
                  final_norm=(layer == depth - 1))
    return xt.reshape(b, s, d)
```

```python
import functools
import math

import jax
import jax.numpy as jnp
from jax import lax
from jax.experimental import pallas as pl
from jax.experimental.pallas import tpu as pltpu

HEAD_DIM = 128
N_SB_HEADS = 8
N_DIFF_HEADS = 4
SB_WIDTH = N_SB_HEADS * HEAD_DIM
DIFF_WIDTH = N_DIFF_HEADS * 2 * HEAD_DIM
ROPE_THETA = 10000.0
RMS_EPS = 1e-6
FFN_RESIDUAL_WEIGHT = 0.5
LOG2E = math.log2(math.e)
Q_SCALE = LOG2E / math.sqrt(HEAD_DIM)

V7X_VMEM_BYTES = 64 * 1024 * 1024
MIB = 1024 * 1024
NEG_BIG = -1e30

F32 = jnp.float32
BF16 = jnp.bfloat16


def _rmsnorm_rows(x, g):
    ms = jnp.mean(x * x, axis=-1, keepdims=True)
    return x * lax.rsqrt(ms + RMS_EPS) * g


def _dot(a, b):
    return jnp.dot(a, b, preferred_element_type=F32)


def _dot_nt(a, b):
    return lax.dot_general(a, b, (((1,), (1,)), ((), ())), preferred_element_type=F32)


def _ffn_kernel(x_ref, g_ref, wg_ref, wu_ref, wd_ref, gf_ref, o_ref, h_ref, *, final_norm):
    j = pl.program_id(1)
    nj = pl.num_programs(1)

    @pl.when(j == 0)
    def _():
        x = x_ref[...]
        h_ref[...] = _rmsnorm_rows(x, g_ref[...]).astype(BF16)
        o_ref[...] = x

    h = h_ref[...]
    gate = _dot(h, wg_ref[...])
    up = _dot(h, wu_ref[...])
    act = (gate / (1.0 + jnp.exp(-gate))) * (up * FFN_RESIDUAL_WEIGHT)
    o_ref[...] += _dot(act.astype(BF16), wd_ref[...])

    if final_norm:
        @pl.when(j == nj - 1)
        def _():
            o_ref[...] = _rmsnorm_rows(o_ref[...], gf_ref[...])


def _ffn(x, g, wg, wu, wd, g_final, *, final_norm, tm=1024, tf=512):
    t, d = x.shape
    d_ff = wg.shape[1]
    assert t % tm == 0 and d_ff % tf == 0
    return pl.pallas_call(
        functools.partial(_ffn_kernel, final_norm=final_norm),
        out_shape=jax.ShapeDtypeStruct((t, d), F32),
        grid=(t // tm, d_ff // tf),
        in_specs=[
            pl.BlockSpec((tm, d), lambda i, j: (i, 0), pipeline_mode=pl.Buffered(1)),
            pl.BlockSpec((1, d), lambda i, j: (0, 0)),
            pl.BlockSpec((d, tf), lambda i, j: (0, j)),
            pl.BlockSpec((d, tf), lambda i, j: (0, j)),
            pl.BlockSpec((tf, d), lambda i, j: (j, 0)),
            pl.BlockSpec((1, d), lambda i, j: (0, 0)),
        ],
        out_specs=pl.BlockSpec((tm, d), lambda i, j: (i, 0)),
        scratch_shapes=[pltpu.VMEM((tm, d), BF16)],
        compiler_params=pltpu.CompilerParams(
            dimension_semantics=("parallel", "arbitrary"),
            vmem_limit_bytes=56 * MIB,
        ),
        name="ffn_swiglu",
    )(x, g, wg, wu, wd, g_final)


def _inproj_kernel(x_ref, g_ref, w_ref, cos_ref, sin_ref, o_ref, h_ref, *, tn):
    j = pl.program_id(1)

    @pl.when(j == 0)
    def _():
        h_ref[...] = _rmsnorm_rows(x_ref[...], g_ref[...]).astype(BF16)

    acc = _dot(h_ref[...], w_ref[...])
    col = j * tn
    sbq_end = SB_WIDTH
    dq_start = 3 * SB_WIDTH
    dk_start = dq_start + DIFF_WIDTH
    dv_start = dk_start + DIFF_WIDTH

    def rope(y):
        cos = cos_ref[...]
        sin = sin_ref[...]
        parts = []
        for c in range(tn // HEAD_DIM):
            yc = y[:, c * HEAD_DIM:(c + 1) * HEAD_DIM]
            parts.append(yc * cos + pltpu.roll(yc, HEAD_DIM // 2, 1) * sin)
        return jnp.concatenate(parts, axis=1)

    @pl.when(col < sbq_end)
    def _():
        o_ref[...] = (acc * Q_SCALE).astype(o_ref.dtype)

    @pl.when(jnp.logical_or(jnp.logical_and(col >= sbq_end, col < dq_start), col >= dv_start))
    def _():
        o_ref[...] = acc.astype(o_ref.dtype)

    @pl.when(jnp.logical_and(col >= dq_start, col < dk_start))
    def _():
        o_ref[...] = (rope(acc) * Q_SCALE).astype(o_ref.dtype)

    @pl.when(jnp.logical_and(col >= dk_start, col < dv_start))
    def _():
        o_ref[...] = rope(acc).astype(o_ref.dtype)


def _inproj(x, g, w, cos, sin, seq, *, tm=1024, tn=512):
    t, d = x.shape
    d_in = w.shape[1]
    assert t % tm == 0 and d_in % tn == 0 and seq % tm == 0 and SB_WIDTH % tn == 0
    pos_blocks = seq // tm
    return pl.pallas_call(
        functools.partial(_inproj_kernel, tn=tn),
        out_shape=jax.ShapeDtypeStruct((t, d_in), BF16),
        grid=(t // tm, d_in // tn),
        in_specs=[
            pl.BlockSpec((tm, d), lambda i, j: (i, 0)),
            pl.BlockSpec((1, d), lambda i, j: (0, 0)),
            pl.BlockSpec((d, tn), lambda i, j: (0, j)),
            pl.BlockSpec((tm, HEAD_DIM), lambda i, j: (i % pos_blocks, 0)),
            pl.BlockSpec((tm, HEAD_DIM), lambda i, j: (i % pos_blocks, 0)),
        ],
        out_specs=pl.BlockSpec((tm, tn), lambda i, j: (i, j)),
        scratch_shapes=[pltpu.VMEM((tm, d), BF16)],
        compiler_params=pltpu.CompilerParams(
            dimension_semantics=("parallel", "arbitrary"),
            vmem_limit_bytes=48 * MIB,
        ),
        name="inproj_rope",
    )(x, g, w, cos, sin)


def _sb_tile(q, k, v, tri, carry, mask):
    z = _dot_nt(q, k)
    sp = jnp.maximum(z, 0.0) + jnp.log(1.0 + jnp.exp2(-jnp.abs(z))) * LOG2E
    if mask is not None:
        sp = jnp.where(mask, sp, 0.0)
    hi = sp.astype(BF16)
    lo = (sp - hi.astype(F32)).astype(BF16)
    incl = _dot(hi, tri) + _dot(lo, tri)
    w = jnp.exp2(z - incl - carry)
    if mask is not None:
        w = jnp.where(mask, w, 0.0)
    out = _dot(w.astype(BF16), v)
    return out, carry + incl[:, 0:1]


def _sb_kernel(q_ref, k_ref, v_ref, tri_ref, o_ref, acc_ref, carry_ref, *, tq, tk):
    i = pl.program_id(2)
    nb = tq // tk
    tri = tri_ref[...]
    acc_ref[...] = jnp.zeros_like(acc_ref)
    carry_ref[...] = jnp.zeros_like(carry_ref)

    for d in reversed(range(nb)):
        r0 = d * tk
        rows = tq - r0
        s0 = pl.multiple_of(i * tq + r0, tk)
        row = lax.broadcasted_iota(jnp.int32, (rows, tk), 0)
        col = lax.broadcasted_iota(jnp.int32, (rows, tk), 1)
        out, carry = _sb_tile(q_ref[0, r0:, :], k_ref[0, pl.ds(s0, tk), :], v_ref[0, pl.ds(s0, tk), :], tri,
                              carry_ref[r0:, :], col < row)
        acc_ref[r0:, :] += out
        carry_ref[r0:, :] = carry

    def body(step, _):
        kb = i * nb - 1 - step
        s0 = pl.multiple_of(kb * tk, tk)
        out, carry = _sb_tile(q_ref[0], k_ref[0, pl.ds(s0, tk), :], v_ref[0, pl.ds(s0, tk), :], tri,
                              carry_ref[...], None)
        acc_ref[...] += out
        carry_ref[...] = carry
        return 0

    lax.fori_loop(0, i * nb, body, 0)
    o_ref[0] = acc_ref[...].astype(o_ref.dtype)


def _sb_attention(proj, tri, *, tq, tk):
    b, s, _ = proj.shape
    assert s % tq == 0 and tq % tk == 0 and tri.shape == (tk, tk)
    qoff, koff, voff = 0, N_SB_HEADS, 2 * N_SB_HEADS
    return pl.pallas_call(
        functools.partial(_sb_kernel, tq=tq, tk=tk),
        out_shape=jax.ShapeDtypeStruct((b, s, SB_WIDTH), BF16),
        grid=(b, N_SB_HEADS, s // tq),
        in_specs=[
            pl.BlockSpec((1, tq, HEAD_DIM), lambda bi, h, i: (bi, i, qoff + h)),
            pl.BlockSpec((1, s, HEAD_DIM), lambda bi, h, i: (bi, 0, koff + h)),
            pl.BlockSpec((1, s, HEAD_DIM), lambda bi, h, i: (bi, 0, voff + h)),
            pl.BlockSpec((tk, tk), lambda bi, h, i: (0, 0)),
        ],
        out_specs=pl.BlockSpec((1, tq, HEAD_DIM), lambda bi, h, i: (bi, i, h)),
        scratch_shapes=[pltpu.VMEM((tq, HEAD_DIM), F32), pltpu.VMEM((tq, 1), F32)],
        compiler_params=pltpu.CompilerParams(
            dimension_semantics=("parallel", "parallel", "arbitrary"),
            vmem_limit_bytes=40 * MIB,
        ),
        name="stickbreak_attn",
    )(proj, proj, proj, tri)


def _diff_update(z, v, m_ref, l_ref, acc_ref, r0):
    m_old = m_ref[r0:, :]
    m_new = jnp.maximum(m_old, jnp.max(z, axis=-1, keepdims=True))
    alpha = jnp.exp2(m_old - m_new)
    p = jnp.exp2(z - m_new)
    l_ref[r0:, :] = alpha * l_ref[r0:, :] + jnp.sum(p, axis=-1, keepdims=True)
    acc_ref[r0:, :] = alpha * acc_ref[r0:, :] + _dot(p.astype(BF16), v)
    m_ref[r0:, :] = m_new


def _diff_kernel(q_ref, k_ref, v_ref, lq1_ref, lk1_ref, lq2_ref, lk2_ref, sub_ref, o_ref,
                 m1_ref, l1_ref, a1_ref, m2_ref, l2_ref, a2_ref, *, tq, tk, lambda_init):
    i = pl.program_id(2)
    nb = tq // tk

    m1_ref[...] = jnp.full_like(m1_ref, NEG_BIG)
    m2_ref[...] = jnp.full_like(m2_ref, NEG_BIG)
    l1_ref[...] = jnp.zeros_like(l1_ref)
    l2_ref[...] = jnp.zeros_like(l2_ref)
    a1_ref[...] = jnp.zeros_like(a1_ref)
    a2_ref[...] = jnp.zeros_like(a2_ref)

    def tile(s0, r0, mask):
        q = q_ref[0, r0:, :]
        k = k_ref[0, pl.ds(s0, tk), :]
        v = v_ref[0, pl.ds(s0, tk), :]
        z1 = _dot_nt(q[:, :HEAD_DIM], k[:, :HEAD_DIM])
        z2 = _dot_nt(q[:, HEAD_DIM:], k[:, HEAD_DIM:])
        if mask is not None:
            z1 = jnp.where(mask, z1, NEG_BIG)
            z2 = jnp.where(mask, z2, NEG_BIG)
        _diff_update(z1, v, m1_ref, l1_ref, a1_ref, r0)
        _diff_update(z2, v, m2_ref, l2_ref, a2_ref, r0)

    for d in reversed(range(nb)):
        r0 = d * tk
        rows = tq - r0
        row = lax.broadcasted_iota(jnp.int32, (rows, tk), 0)
        col = lax.broadcasted_iota(jnp.int32, (rows, tk), 1)
        tile(pl.multiple_of(i * tq + r0, tk), r0, col <= row)

    def body(step, _):
        tile(pl.multiple_of(step * tk, tk), 0, None)
        return 0

    lax.fori_loop(0, i * nb, body, 0)

    lam = (jnp.exp(jnp.sum(lq1_ref[...] * lk1_ref[...])) - jnp.exp(jnp.sum(lq2_ref[...] * lk2_ref[...]))
           + lambda_init)
    o = a1_ref[...] / l1_ref[...] - lam * (a2_ref[...] / l2_ref[...])
    o = _rmsnorm_rows(o, sub_ref[...]) * (1.0 - lambda_init)
    o_ref[0] = o.astype(o_ref.dtype)


def _diff_attention(proj, lq1, lk1, lq2, lk2, subln, lambda_init, *, tq, tk):
    b, s, _ = proj.shape
    assert s % tq == 0 and tq % tk == 0
    dv = 2 * HEAD_DIM
    qoff = 3 * SB_WIDTH // dv
    koff = qoff + N_DIFF_HEADS
    voff = koff + N_DIFF_HEADS
    vec = pl.BlockSpec((1, HEAD_DIM), lambda bi, h, i: (0, 0))
    return pl.pallas_call(
        functools.partial(_diff_kernel, tq=tq, tk=tk, lambda_init=lambda_init),
        out_shape=jax.ShapeDtypeStruct((b, s, DIFF_WIDTH), BF16),
        grid=(b, N_DIFF_HEADS, s // tq),
        in_specs=[
            pl.BlockSpec((1, tq, dv), lambda bi, h, i: (bi, i, qoff + h)),
            pl.BlockSpec((1, s, dv), lambda bi, h, i: (bi, 0, koff + h)),
            pl.BlockSpec((1, s, dv), lambda bi, h, i: (bi, 0, voff + h)),
            vec, vec, vec, vec,
            pl.BlockSpec((1, dv), lambda bi, h, i: (0, 0)),
        ],
        out_specs=pl.BlockSpec((1, tq, dv), lambda bi, h, i: (bi, i, h)),
        scratch_shapes=[
            pltpu.VMEM((tq, 1), F32), pltpu.VMEM((tq, 1), F32), pltpu.VMEM((tq, dv), F32),
            pltpu.VMEM((tq, 1), F32), pltpu.VMEM((tq, 1), F32), pltpu.VMEM((tq, dv), F32),
        ],
        compiler_params=pltpu.CompilerParams(
            dimension_semantics=("parallel", "parallel", "arbitrary"),
            vmem_limit_bytes=40 * MIB,
        ),
        name="diff_attn",
    )(proj, proj, proj, lq1, lk1, lq2, lk2, subln)


def _outproj_kernel(x_ref, sb_ref, df_ref, w1_ref, w2_ref, o_ref):
    o_ref[...] = x_ref[...] + _dot(sb_ref[...], w1_ref[...]) + _dot(df_ref[...], w2_ref[...])


def _outproj(x, sb, df, w_out, *, tm=1024, tn=1024):
    t, d = x.shape
    assert t % tm == 0 and d % tn == 0
    k1 = sb.shape[1]
    k2 = df.shape[1]
    assert k1 % tn == 0 or tn % k1 == 0
    return pl.pallas_call(
        _outproj_kernel,
        out_shape=jax.ShapeDtypeStruct((t, d), F32),
        grid=(t // tm, d // tn),
        in_specs=[
            pl.BlockSpec((tm, tn), lambda i, j: (i, j)),
            pl.BlockSpec((tm, k1), lambda i, j: (i, 0)),
            pl.BlockSpec((tm, k2), lambda i, j: (i, 0)),
            pl.BlockSpec((k1, tn), lambda i, j: (0, j)),
            pl.BlockSpec((k2, tn), lambda i, j: (1, j)),
        ],
        out_specs=pl.BlockSpec((tm, tn), lambda i, j: (i, j)),
        compiler_params=pltpu.CompilerParams(
            dimension_semantics=("parallel", "arbitrary"),
            vmem_limit_bytes=48 * MIB,
        ),
        name="outproj_residual",
    )(x, sb, df, w_out, w_out)


def _rope_tables(seq):
    pos = jnp.arange(seq, dtype=F32)
    inv_freq = ROPE_THETA ** (-jnp.arange(0, HEAD_DIM, 2, dtype=F32) / HEAD_DIM)
    ang = pos[:, None] * inv_freq[None, :]
    ang = jnp.concatenate([ang, ang], axis=-1)
    sign = jnp.concatenate([-jnp.ones((HEAD_DIM // 2,), F32), jnp.ones((HEAD_DIM // 2,), F32)])
    return jnp.cos(ang), jnp.sin(ang) * sign[None, :]


def kernel(x, norm_ffn1, w_ffn1_gate, w_ffn1_up, w_ffn1_down, norm_mix, w_in, lambda_q1, lambda_k1, lambda_q2, lambda_k2, diff_subln, w_out, norm_ffn2, w_ffn2_gate, w_ffn2_up, w_ffn2_down, norm_final):
    b, s, d = x.shape
    depth = w_in.shape[0]
    cos, sin = _rope_tables(s)
    tq, tk = 1024, 256
    tri = (lax.broadcasted_iota(jnp.int32, (tk, tk), 0) >= lax.broadcasted_iota(jnp.int32, (tk, tk), 1)).astype(BF16)
    g_final = norm_final.reshape(1, d)

    xt = x.reshape(b * s, d)
    for layer in range(depth):
        lambda_init = 0.8 - 0.6 * math.exp(-0.3 * layer)
        xt = _ffn(xt, norm_ffn1[layer].reshape(1, d), w_ffn1_gate[layer].astype(BF16),
                  w_ffn1_up[layer].astype(BF16), w_ffn1_down[layer].astype(BF16), g_final, final_norm=False)
        proj = _inproj(xt, norm_mix[layer].reshape(1, d), w_in[layer].astype(BF16), cos, sin, s)
        proj = proj.reshape(b, s, -1)
        sb = _sb_attention(proj, tri, tq=tq, tk=tk)
        df = _diff_attention(proj, lambda_q1[layer].reshape(1, -1), lambda_k1[layer].reshape(1, -1),
                             lambda_q2[layer].reshape(1, -1), lambda_k2[layer].reshape(1, -1),
                             diff_subln[layer].reshape(1, -1), lambda_init, tq=tq, tk=tk)
        xt = _outproj(xt, sb.reshape(b * s, -1), df.reshape(b * s, -1), w_out[layer].astype(BF16))
        xt = _ffn(xt, norm_ffn2[layer].reshape(1, d), w_ffn2_gate[layer].astype(BF16),
                  w_ffn2_up[layer].astype(BF16), w_ffn2_down[layer].astype(BF16), g_final,
                  final_norm=(layer == depth - 1))
    return xt.reshape(b, s, d)
```

```python
import functools
import math

import jax
import jax.numpy as jnp
from jax import lax
from jax.experimental import pallas as pl
from jax.experimental.pallas import tpu as pltpu

HEAD_DIM = 128
N_SB_HEADS = 8
N_DIFF_HEADS = 4
SB_WIDTH = N_SB_HEADS * HEAD_DIM
DIFF_WIDTH = N_DIFF_HEADS * 2 * HEAD_DIM
ROPE_THETA = 10000.0
RMS_EPS = 1e-6
FFN_RESIDUAL_WEIGHT = 0.5
LOG2E = math.log2(math.e)
Q_SCALE = LOG2E / math.sqrt(HEAD_DIM)

DIFF_BAND_KEYS = 1024
SB_UNROLL = 4
V7X_VMEM_BYTES = 64 * 1024 * 1024
MIB = 1024 * 1024
NEG_BIG = -1e30

F32 = jnp.float32
BF16 = jnp.bfloat16


def _rmsnorm_rows(x, g):
    ms = jnp.mean(x * x, axis=-1, keepdims=True)
    return x * lax.rsqrt(ms + RMS_EPS) * g


def _dot(a, b):
    return jnp.dot(a, b, preferred_element_type=F32)


def _dot_nt(a, b):
    return lax.dot_general(a, b, (((1,), (1,)), ((), ())), preferred_element_type=F32)


def _ffn_kernel(x_ref, g_ref, wg_ref, wu_ref, wd_ref, gf_ref, o_ref, h_ref, *, final_norm):
    j = pl.program_id(1)
    nj = pl.num_programs(1)

    @pl.when(j == 0)
    def _():
        x = x_ref[...]
        h_ref[...] = _rmsnorm_rows(x, g_ref[...]).astype(BF16)
        o_ref[...] = x

    h = h_ref[...]
    gate = _dot(h, wg_ref[...])
    up = _dot(h, wu_ref[...])
    act = (gate / (1.0 + jnp.exp(-gate))) * (up * FFN_RESIDUAL_WEIGHT)
    o_ref[...] += _dot(act.astype(BF16), wd_ref[...])

    if final_norm:
        @pl.when(j == nj - 1)
        def _():
            o_ref[...] = _rmsnorm_rows(o_ref[...], gf_ref[...])


def _ffn(x, g, wg, wu, wd, g_final, *, final_norm, tm=1024, tf=512):
    t, d = x.shape
    d_ff = wg.shape[1]
    assert t % tm == 0 and d_ff % tf == 0
    return pl.pallas_call(
        functools.partial(_ffn_kernel, final_norm=final_norm),
        out_shape=jax.ShapeDtypeStruct((t, d), F32),
        grid=(t // tm, d_ff // tf),
        in_specs=[
            pl.BlockSpec((tm, d), lambda i, j: (i, 0), pipeline_mode=pl.Buffered(1)),
            pl.BlockSpec((1, d), lambda i, j: (0, 0)),
            pl.BlockSpec((d, tf), lambda i, j: (0, j)),
            pl.BlockSpec((d, tf), lambda i, j: (0, j)),
            pl.BlockSpec((tf, d), lambda i, j: (j, 0)),
            pl.BlockSpec((1, d), lambda i, j: (0, 0)),
        ],
        out_specs=pl.BlockSpec((tm, d), lambda i, j: (i, 0)),
        scratch_shapes=[pltpu.VMEM((tm, d), BF16)],
        compiler_params=pltpu.CompilerParams(
            dimension_semantics=("parallel", "arbitrary"),
            vmem_limit_bytes=56 * MIB,
        ),
        name="ffn_swiglu",
    )(x, g, wg, wu, wd, g_final)


def _inproj_kernel(x_ref, g_ref, w_ref, cos_ref, sin_ref, o_ref, h_ref, *, tn):
    j = pl.program_id(1)

    @pl.when(j == 0)
    def _():
        h_ref[...] = _rmsnorm_rows(x_ref[...], g_ref[...]).astype(BF16)

    acc = _dot(h_ref[...], w_ref[...])
    col = j * tn
    sbq_end = SB_WIDTH
    dq_start = 3 * SB_WIDTH
    dk_start = dq_start + DIFF_WIDTH
    dv_start = dk_start + DIFF_WIDTH

    def rope(y):
        cos = cos_ref[...]
        sin = sin_ref[...]
        parts = []
        for c in range(tn // HEAD_DIM):
            yc = y[:, c * HEAD_DIM:(c + 1) * HEAD_DIM]
            parts.append(yc * cos + pltpu.roll(yc, HEAD_DIM // 2, 1) * sin)
        return jnp.concatenate(parts, axis=1)

    @pl.when(col < sbq_end)
    def _():
        o_ref[...] = (acc * Q_SCALE).astype(o_ref.dtype)

    @pl.when(jnp.logical_or(jnp.logical_and(col >= sbq_end, col < dq_start), col >= dv_start))
    def _():
        o_ref[...] = acc.astype(o_ref.dtype)

    @pl.when(jnp.logical_and(col >= dq_start, col < dk_start))
    def _():
        o_ref[...] = (rope(acc) * Q_SCALE).astype(o_ref.dtype)

    @pl.when(jnp.logical_and(col >= dk_start, col < dv_start))
    def _():
        o_ref[...] = rope(acc).astype(o_ref.dtype)


def _inproj(x, g, w, cos, sin, seq, *, tm=1024, tn=512):
    t, d = x.shape
    d_in = w.shape[1]
    assert t % tm == 0 and d_in % tn == 0 and seq % tm == 0 and SB_WIDTH % tn == 0
    pos_blocks = seq // tm
    return pl.pallas_call(
        functools.partial(_inproj_kernel, tn=tn),
        out_shape=jax.ShapeDtypeStruct((t, d_in), BF16),
        grid=(t // tm, d_in // tn),
        in_specs=[
            pl.BlockSpec((tm, d), lambda i, j: (i, 0)),
            pl.BlockSpec((1, d), lambda i, j: (0, 0)),
            pl.BlockSpec((d, tn), lambda i, j: (0, j)),
            pl.BlockSpec((tm, HEAD_DIM), lambda i, j: (i % pos_blocks, 0)),
            pl.BlockSpec((tm, HEAD_DIM), lambda i, j: (i % pos_blocks, 0)),
        ],
        out_specs=pl.BlockSpec((tm, tn), lambda i, j: (i, j)),
        scratch_shapes=[pltpu.VMEM((tm, d), BF16)],
        compiler_params=pltpu.CompilerParams(
            dimension_semantics=("parallel", "arbitrary"),
            vmem_limit_bytes=48 * MIB,
        ),
        name="inproj_rope",
    )(x, g, w, cos, sin)


def _neg_abs(z):
    return pltpu.bitcast(pltpu.bitcast(z, jnp.uint32) | jnp.uint32(0x80000000), F32)


def _mask_top_rows(x, mask, fill):
    n = mask.shape[0]
    top = jnp.where(mask, x[:n], fill)
    return top if n == x.shape[0] else jnp.concatenate([top, x[n:]], axis=0)


def _sb_tile(q, k, v, tri2, carry, mask):
    z = _dot_nt(q, k)
    sp = jnp.maximum(z, 0.0) + jnp.log(1.0 + jnp.exp2(_neg_abs(z))) * LOG2E
    if mask is not None:
        sp = _mask_top_rows(sp, mask, 0.0)
    hi = sp.astype(BF16)
    lo = (sp - hi.astype(F32)).astype(BF16)
    incl = _dot(jnp.concatenate([hi, lo], axis=1), tri2)
    w = jnp.exp2(z - incl - carry)
    if mask is not None:
        w = _mask_top_rows(w, mask, 0.0)
    out = _dot(w.astype(BF16), v)
    return out, carry + incl[:, 0:1]


def _sb_kernel(q_ref, k_ref, v_ref, tri_ref, o_ref, acc_ref, carry_ref, *, tq, tk, unroll):
    i = pl.program_id(2)
    nb = tq // tk
    assert nb % unroll == 0
    tri = tri_ref[...]

    row = lax.broadcasted_iota(jnp.int32, (tk, tk), 0)
    col = lax.broadcasted_iota(jnp.int32, (tk, tk), 1)
    strict_lower = col < row
    acc = None
    carry = None
    for d in reversed(range(nb)):
        r0 = d * tk
        s0 = pl.multiple_of(i * tq + r0, tk)
        zero_c = jnp.zeros((tk, 1), F32)
        carry_in = zero_c if carry is None else jnp.concatenate([zero_c, carry], axis=0)
        out, carry = _sb_tile(q_ref[0, r0:, :], k_ref[0, pl.ds(s0, tk), :], v_ref[0, pl.ds(s0, tk), :], tri,
                              carry_in, strict_lower)
        acc = out if acc is None else out + jnp.concatenate([jnp.zeros((tk, HEAD_DIM), F32), acc], axis=0)
    acc_ref[...] = acc
    carry_ref[...] = carry

    def body(step, _):
        q = q_ref[0]
        carry = carry_ref[...]
        total = None
        for u in range(unroll):
            s0 = pl.multiple_of(i * tq - (step * unroll + u + 1) * tk, tk)
            out, carry = _sb_tile(q, k_ref[0, pl.ds(s0, tk), :], v_ref[0, pl.ds(s0, tk), :], tri, carry, None)
            total = out if total is None else total + out
        acc_ref[...] += total
        carry_ref[...] = carry
        return 0

    lax.fori_loop(0, i * (nb // unroll), body, 0)
    o_ref[0] = acc_ref[...].astype(o_ref.dtype)


def _sb_attention(proj, tri, *, tq, tk):
    b, s, _ = proj.shape
    assert s % tq == 0 and tq % tk == 0 and tri.shape == (2 * tk, tk)
    qoff, koff, voff = 0, N_SB_HEADS, 2 * N_SB_HEADS
    return pl.pallas_call(
        functools.partial(_sb_kernel, tq=tq, tk=tk, unroll=SB_UNROLL),
        out_shape=jax.ShapeDtypeStruct((b, s, SB_WIDTH), BF16),
        grid=(b, N_SB_HEADS, s // tq),
        in_specs=[
            pl.BlockSpec((1, tq, HEAD_DIM), lambda bi, h, i: (bi, i, qoff + h)),
            pl.BlockSpec((1, s, HEAD_DIM), lambda bi, h, i: (bi, 0, koff + h)),
            pl.BlockSpec((1, s, HEAD_DIM), lambda bi, h, i: (bi, 0, voff + h)),
            pl.BlockSpec((2 * tk, tk), lambda bi, h, i: (0, 0)),
        ],
        out_specs=pl.BlockSpec((1, tq, HEAD_DIM), lambda bi, h, i: (bi, i, h)),
        scratch_shapes=[pltpu.VMEM((tq, HEAD_DIM), F32), pltpu.VMEM((tq, 1), F32)],
        compiler_params=pltpu.CompilerParams(
            dimension_semantics=("parallel", "parallel", "arbitrary"),
            vmem_limit_bytes=40 * MIB,
        ),
        name="stickbreak_attn",
    )(proj, proj, proj, tri)


def _diff_update(z, v, m_ref, l_ref, acc_ref, r0):
    m_old = m_ref[r0:, :]
    m_new = jnp.maximum(m_old, jnp.max(z, axis=-1, keepdims=True))
    alpha = jnp.exp2(m_old - m_new)
    p = jnp.exp2(z - m_new)
    l_ref[r0:, :] = alpha * l_ref[r0:, :] + jnp.sum(p, axis=-1, keepdims=True)
    acc_ref[r0:, :] = alpha * acc_ref[r0:, :] + _dot(p.astype(BF16), v)
    m_ref[r0:, :] = m_new


def _diff_kernel(q_ref, k_ref, v_ref, lq1_ref, lk1_ref, lq2_ref, lk2_ref, sub_ref, o_ref,
                 m1_ref, l1_ref, a1_ref, m2_ref, l2_ref, a2_ref, *, tq, tk, lambda_init):
    i = pl.program_id(2)
    nb = tq // tk

    m1_ref[...] = jnp.full_like(m1_ref, NEG_BIG)
    m2_ref[...] = jnp.full_like(m2_ref, NEG_BIG)
    l1_ref[...] = jnp.zeros_like(l1_ref)
    l2_ref[...] = jnp.zeros_like(l2_ref)
    a1_ref[...] = jnp.zeros_like(a1_ref)
    a2_ref[...] = jnp.zeros_like(a2_ref)

    def tile(s0, r0, width, mask):
        q = q_ref[0, r0:, :]
        k = k_ref[0, pl.ds(s0, width), :]
        v = v_ref[0, pl.ds(s0, width), :]
        z1 = _dot_nt(q[:, :HEAD_DIM], k[:, :HEAD_DIM])
        z2 = _dot_nt(q[:, HEAD_DIM:], k[:, HEAD_DIM:])
        if mask is not None:
            z1 = _mask_top_rows(z1, mask, NEG_BIG)
            z2 = _mask_top_rows(z2, mask, NEG_BIG)
        _diff_update(z1, v, m1_ref, l1_ref, a1_ref, r0)
        _diff_update(z2, v, m2_ref, l2_ref, a2_ref, r0)

    row = lax.broadcasted_iota(jnp.int32, (tk, tk), 0)
    col = lax.broadcasted_iota(jnp.int32, (tk, tk), 1)
    lower = col <= row
    for d in reversed(range(nb)):
        r0 = d * tk
        tile(pl.multiple_of(i * tq + r0, tk), r0, tk, lower)

    def body(step, _):
        tile(pl.multiple_of(step * tq, tq), 0, tq, None)
        return 0

    lax.fori_loop(0, i, body, 0)

    lam = (jnp.exp(jnp.sum(lq1_ref[...] * lk1_ref[...])) - jnp.exp(jnp.sum(lq2_ref[...] * lk2_ref[...]))
           + lambda_init)
    o = a1_ref[...] / l1_ref[...] - lam * (a2_ref[...] / l2_ref[...])
    o = _rmsnorm_rows(o, sub_ref[...]) * (1.0 - lambda_init)
    o_ref[0] = o.astype(o_ref.dtype)


def _diff_attention(proj, lq1, lk1, lq2, lk2, subln, lambda_init, *, tq, tk):
    b, s, _ = proj.shape
    assert s % tq == 0 and tq % tk == 0
    dv = 2 * HEAD_DIM
    qoff = 3 * SB_WIDTH // dv
    koff = qoff + N_DIFF_HEADS
    voff = koff + N_DIFF_HEADS
    vec = pl.BlockSpec((1, HEAD_DIM), lambda bi, h, i: (0, 0))
    return pl.pallas_call(
        functools.partial(_diff_kernel, tq=tq, tk=tk, lambda_init=lambda_init),
        out_shape=jax.ShapeDtypeStruct((b, s, DIFF_WIDTH), BF16),
        grid=(b, N_DIFF_HEADS, s // tq),
        in_specs=[
            pl.BlockSpec((1, tq, dv), lambda bi, h, i: (bi, i, qoff + h)),
            pl.BlockSpec((1, s, dv), lambda bi, h, i: (bi, 0, koff + h)),
            pl.BlockSpec((1, s, dv), lambda bi, h, i: (bi, 0, voff + h)),
            vec, vec, vec, vec,
            pl.BlockSpec((1, dv), lambda bi, h, i: (0, 0)),
        ],
        out_specs=pl.BlockSpec((1, tq, dv), lambda bi, h, i: (bi, i, h)),
        scratch_shapes=[
            pltpu.VMEM((tq, 1), F32), pltpu.VMEM((tq, 1), F32), pltpu.VMEM((tq, dv), F32),
            pltpu.VMEM((tq, 1), F32), pltpu.VMEM((tq, 1), F32), pltpu.VMEM((tq, dv), F32),
        ],
        compiler_params=pltpu.CompilerParams(
            dimension_semantics=("parallel", "parallel", "arbitrary"),
            vmem_limit_bytes=40 * MIB,
        ),
        name="diff_attn",
    )(proj, proj, proj, lq1, lk1, lq2, lk2, subln)


def _outproj_kernel(x_ref, sb_ref, df_ref, w1_ref, w2_ref, o_ref):
    o_ref[...] = x_ref[...] + _dot(sb_ref[...], w1_ref[...]) + _dot(df_ref[...], w2_ref[...])


def _outproj(x, sb, df, w_out, *, tm=1024, tn=1024):
    t, d = x.shape
    assert t % tm == 0 and d % tn == 0
    k1 = sb.shape[1]
    k2 = df.shape[1]
    assert k1 % tn == 0 or tn % k1 == 0
    return pl.pallas_call(
        _outproj_kernel,
        out_shape=jax.ShapeDtypeStruct((t, d), F32),
        grid=(t // tm, d // tn),
        in_specs=[
            pl.BlockSpec((tm, tn), lambda i, j: (i, j)),
            pl.BlockSpec((tm, k1), lambda i, j: (i, 0)),
            pl.BlockSpec((tm, k2), lambda i, j: (i, 0)),
            pl.BlockSpec((k1, tn), lambda i, j: (0, j)),
            pl.BlockSpec((k2, tn), lambda i, j: (1, j)),
        ],
        out_specs=pl.BlockSpec((tm, tn), lambda i, j: (i, j)),
        compiler_params=pltpu.CompilerParams(
            dimension_semantics=("parallel", "arbitrary"),
            vmem_limit_bytes=48 * MIB,
        ),
        name="outproj_residual",
    )(x, sb, df, w_out, w_out)


def _rope_tables(seq):
    pos = jnp.arange(seq, dtype=F32)
    inv_freq = ROPE_THETA ** (-jnp.arange(0, HEAD_DIM, 2, dtype=F32) / HEAD_DIM)
    ang = pos[:, None] * inv_freq[None, :]
    ang = jnp.concatenate([ang, ang], axis=-1)
    sign = jnp.concatenate([-jnp.ones((HEAD_DIM // 2,), F32), jnp.ones((HEAD_DIM // 2,), F32)])
    return jnp.cos(ang), jnp.sin(ang) * sign[None, :]


def kernel(x, norm_ffn1, w_ffn1_gate, w_ffn1_up, w_ffn1_down, norm_mix, w_in, lambda_q1, lambda_k1, lambda_q2, lambda_k2, diff_subln, w_out, norm_ffn2, w_ffn2_gate, w_ffn2_up, w_ffn2_down, norm_final):
    b, s, d = x.shape
    depth = w_in.shape[0]
    cos, sin = _rope_tables(s)
    tq, tk = 1024, 256
    tri = (lax.broadcasted_iota(jnp.int32, (tk, tk), 0) >= lax.broadcasted_iota(jnp.int32, (tk, tk), 1)).astype(BF16)
    tri = jnp.concatenate([tri, tri], axis=0)
    g_final = norm_final.reshape(1, d)

    xt = x.reshape(b * s, d)
    for layer in range(depth):
        lambda_init = 0.8 - 0.6 * math.exp(-0.3 * layer)
        xt = _ffn(xt, norm_ffn1[layer].reshape(1, d), w_ffn1_gate[layer].astype(BF16),
                  w_ffn1_up[layer].astype(BF16), w_ffn1_down[layer].astype(BF16), g_final, final_norm=False)
        proj = _inproj(xt, norm_mix[layer].reshape(1, d), w_in[layer].astype(BF16), cos, sin, s)
        proj = proj.reshape(b, s, -1)
        sb = _sb_attention(proj, tri, tq=tq, tk=tk)
        df = _diff_attention(proj, lambda_q1[layer].reshape(1, -1), lambda_k1[layer].reshape(1, -1),
                             lambda_q2[layer].reshape(1, -1), lambda_k2[layer].reshape(1, -1),
                             diff_subln[layer].reshape(1, -1), lambda_init, tq=tq, tk=DIFF_BAND_KEYS)
        xt = _outproj(xt, sb.reshape(b * s, -1), df.reshape(b * s, -1), w_out[layer].astype(BF16))
        xt = _ffn(xt, norm_ffn2[layer].reshape(1, d), w_ffn2_gate[layer].astype(BF16),
                  w_ffn2_up[layer].astype(BF16), w_ffn2_down[layer].astype(BF16), g_final,
                  final_norm=(layer == depth - 1))
    return xt.reshape(b, s, d)
```

```python
import functools
import math

import jax
import jax.numpy as jnp
from jax import lax
from jax.experimental import pallas as pl
from jax.experimental.pallas import tpu as pltpu

HEAD_DIM = 128
N_SB_HEADS = 8
N_DIFF_HEADS = 4
SB_WIDTH = N_SB_HEADS * HEAD_DIM
DIFF_WIDTH = N_DIFF_HEADS * 2 * HEAD_DIM
ROPE_THETA = 10000.0
RMS_EPS = 1e-6
FFN_RESIDUAL_WEIGHT = 0.5
LOG2E = math.log2(math.e)
Q_SCALE = LOG2E / math.sqrt(HEAD_DIM)

DIFF_BAND_KEYS = 1024
SB_UNROLL = 4
V7X_VMEM_BYTES = 64 * 1024 * 1024
MIB = 1024 * 1024
NEG_BIG = -1e30

F32 = jnp.float32
BF16 = jnp.bfloat16


def _rmsnorm_rows(x, g):
    ms = jnp.mean(x * x, axis=-1, keepdims=True)
    return x * lax.rsqrt(ms + RMS_EPS) * g


def _dot(a, b):
    return jnp.dot(a, b, preferred_element_type=F32)


def _dot_nt(a, b):
    return lax.dot_general(a, b, (((1,), (1,)), ((), ())), preferred_element_type=F32)


def _ffn_kernel(x_ref, g_ref, wg_ref, wu_ref, wd_ref, gf_ref, o_ref, *rest, final_norm, emit_bf16):
    h_ref = rest[-1]
    j = pl.program_id(1)
    nj = pl.num_programs(1)

    @pl.when(j == 0)
    def _():
        x = x_ref[...]
        h_ref[...] = _rmsnorm_rows(x, g_ref[...]).astype(BF16)
        o_ref[...] = x

    wg = wg_ref[...].astype(BF16)
    wu = wu_ref[...].astype(BF16)
    wd = wd_ref[...].astype(BF16)
    if emit_bf16:
        wg16_ref, wu16_ref, wd16_ref = rest[:3]
        wg16_ref[...] = wg
        wu16_ref[...] = wu
        wd16_ref[...] = wd

    h = h_ref[...]
    gate = _dot(h, wg)
    up = _dot(h, wu)
    act = (gate / (1.0 + jnp.exp(-gate))) * (up * FFN_RESIDUAL_WEIGHT)
    o_ref[...] += _dot(act.astype(BF16), wd)

    if final_norm:
        @pl.when(j == nj - 1)
        def _():
            o_ref[...] = _rmsnorm_rows(o_ref[...], gf_ref[...])


def _ffn_call(x, g, wg, wu, wd, g_final, *, layer, final_norm, row_block0, n_row_blocks, tm, tf):
    t, d = x.shape
    emit_bf16 = layer is not None
    d_ff = wg.shape[-1]
    assert d_ff % tf == 0 and (row_block0 + n_row_blocks) * tm <= t
    if emit_bf16:
        w_in_specs = [
            pl.BlockSpec((None, d, tf), lambda i, j: (layer, 0, j)),
            pl.BlockSpec((None, d, tf), lambda i, j: (layer, 0, j)),
            pl.BlockSpec((None, tf, d), lambda i, j: (layer, j, 0)),
        ]
    else:
        w_in_specs = [
            pl.BlockSpec((d, tf), lambda i, j: (0, j)),
            pl.BlockSpec((d, tf), lambda i, j: (0, j)),
            pl.BlockSpec((tf, d), lambda i, j: (j, 0)),
        ]
    out_shape = [jax.ShapeDtypeStruct((t, d), F32)]
    out_specs = [pl.BlockSpec((tm, d), lambda i, j: (i + row_block0, 0))]
    if emit_bf16:
        out_shape += [jax.ShapeDtypeStruct((d, d_ff), BF16), jax.ShapeDtypeStruct((d, d_ff), BF16),
                      jax.ShapeDtypeStruct((d_ff, d), BF16)]
        out_specs += [pl.BlockSpec((d, tf), lambda i, j: (0, j)), pl.BlockSpec((d, tf), lambda i, j: (0, j)),
                      pl.BlockSpec((tf, d), lambda i, j: (j, 0))]
    res = pl.pallas_call(
        functools.partial(_ffn_kernel, final_norm=final_norm, emit_bf16=emit_bf16),
        out_shape=out_shape,
        grid=(n_row_blocks, d_ff // tf),
        in_specs=[
            pl.BlockSpec((tm, d), lambda i, j: (i + row_block0, 0), pipeline_mode=pl.Buffered(1)),
            pl.BlockSpec((1, d), lambda i, j: (0, 0)),
            *w_in_specs,
            pl.BlockSpec((1, d), lambda i, j: (0, 0)),
        ],
        out_specs=out_specs,
        scratch_shapes=[pltpu.VMEM((tm, d), BF16)],
        input_output_aliases={0: 0},
        compiler_params=pltpu.CompilerParams(
            dimension_semantics=("parallel", "arbitrary"),
            vmem_limit_bytes=56 * MIB,
        ),
        name="ffn_swiglu_cast" if emit_bf16 else "ffn_swiglu",
    )(x, g, wg, wu, wd, g_final)
    return res


def _ffn(x, g, wg32, wu32, wd32, g_final, layer, *, final_norm, tm=1024, tf=512, tf_first=256):
    t, _ = x.shape
    x, wg, wu, wd = _ffn_call(x, g, wg32, wu32, wd32, g_final, layer=layer, final_norm=final_norm,
                              row_block0=0, n_row_blocks=1, tm=tm, tf=tf_first)
    (x,) = _ffn_call(x, g, wg, wu, wd, g_final, layer=None, final_norm=final_norm,
                     row_block0=1, n_row_blocks=t // tm - 1, tm=tm, tf=tf)
    return x


def _inproj_kernel(x_ref, g_ref, w_ref, cos_ref, sin_ref, o_ref, h_ref, *, tn):
    j = pl.program_id(1)

    @pl.when(j == 0)
    def _():
        h_ref[...] = _rmsnorm_rows(x_ref[...], g_ref[...]).astype(BF16)

    acc = _dot(h_ref[...], w_ref[...])
    col = j * tn
    sbq_end = SB_WIDTH
    dq_start = 3 * SB_WIDTH
    dk_start = dq_start + DIFF_WIDTH
    dv_start = dk_start + DIFF_WIDTH

    def rope(y):
        cos = cos_ref[...]
        sin = sin_ref[...]
        parts = []
        for c in range(tn // HEAD_DIM):
            yc = y[:, c * HEAD_DIM:(c + 1) * HEAD_DIM]
            parts.append(yc * cos + pltpu.roll(yc, HEAD_DIM // 2, 1) * sin)
        return jnp.concatenate(parts, axis=1)

    @pl.when(col < sbq_end)
    def _():
        o_ref[...] = (acc * Q_SCALE).astype(o_ref.dtype)

    @pl.when(jnp.logical_or(jnp.logical_and(col >= sbq_end, col < dq_start), col >= dv_start))
    def _():
        o_ref[...] = acc.astype(o_ref.dtype)

    @pl.when(jnp.logical_and(col >= dq_start, col < dk_start))
    def _():
        o_ref[...] = (rope(acc) * Q_SCALE).astype(o_ref.dtype)

    @pl.when(jnp.logical_and(col >= dk_start, col < dv_start))
    def _():
        o_ref[...] = rope(acc).astype(o_ref.dtype)


def _inproj(x, g, w, cos, sin, seq, *, tm=1024, tn=1024):
    t, d = x.shape
    d_in = w.shape[1]
    assert t % tm == 0 and d_in % tn == 0 and seq % tm == 0 and SB_WIDTH % tn == 0
    pos_blocks = seq // tm
    return pl.pallas_call(
        functools.partial(_inproj_kernel, tn=tn),
        out_shape=jax.ShapeDtypeStruct((t, d_in), BF16),
        grid=(t // tm, d_in // tn),
        in_specs=[
            pl.BlockSpec((tm, d), lambda i, j: (i, 0)),
            pl.BlockSpec((1, d), lambda i, j: (0, 0)),
            pl.BlockSpec((d, tn), lambda i, j: (0, j)),
            pl.BlockSpec((tm, HEAD_DIM), lambda i, j: (i % pos_blocks, 0)),
            pl.BlockSpec((tm, HEAD_DIM), lambda i, j: (i % pos_blocks, 0)),
        ],
        out_specs=pl.BlockSpec((tm, tn), lambda i, j: (i, j)),
        scratch_shapes=[pltpu.VMEM((tm, d), BF16)],
        compiler_params=pltpu.CompilerParams(
            dimension_semantics=("parallel", "arbitrary"),
            vmem_limit_bytes=48 * MIB,
        ),
        name="inproj_rope",
    )(x, g, w, cos, sin)


def _neg_abs(z):
    return pltpu.bitcast(pltpu.bitcast(z, jnp.uint32) | jnp.uint32(0x80000000), F32)


def _mask_top_rows(x, mask, fill):
    n = mask.shape[0]
    top = jnp.where(mask, x[:n], fill)
    return top if n == x.shape[0] else jnp.concatenate([top, x[n:]], axis=0)


def _sb_tile(q, k, v, tri, carry, mask):
    z = _dot_nt(q, k)
    sp = jnp.maximum(z, 0.0) + jnp.log(1.0 + jnp.exp2(_neg_abs(z))) * LOG2E
    if mask is not None:
        sp = _mask_top_rows(sp, mask, 0.0)
    excl = _dot(sp.astype(BF16), tri)
    w = jnp.exp2((z - sp) - excl - carry)
    if mask is not None:
        w = _mask_top_rows(w, mask, 0.0)
    out = _dot(w.astype(BF16), v)
    return out, carry + (excl[:, 0:1] + sp[:, 0:1])


def _sb_kernel(q_ref, k_ref, v_ref, tri_ref, o_ref, acc_ref, carry_ref, *, tq, tk, unroll):
    i = pl.program_id(2)
    nb = tq // tk
    assert nb % unroll == 0
    tri = tri_ref[...]

    row = lax.broadcasted_iota(jnp.int32, (tk, tk), 0)
    col = lax.broadcasted_iota(jnp.int32, (tk, tk), 1)
    strict_lower = col < row
    acc = None
    carry = None
    for d in reversed(range(nb)):
        r0 = d * tk
        s0 = pl.multiple_of(i * tq + r0, tk)
        zero_c = jnp.zeros((tk, 1), F32)
        carry_in = zero_c if carry is None else jnp.concatenate([zero_c, carry], axis=0)
        out, carry = _sb_tile(q_ref[0, r0:, :], k_ref[0, pl.ds(s0, tk), :], v_ref[0, pl.ds(s0, tk), :], tri,
                              carry_in, strict_lower)
        acc = out if acc is None else out + jnp.concatenate([jnp.zeros((tk, HEAD_DIM), F32), acc], axis=0)
    acc_ref[...] = acc
    carry_ref[...] = carry

    def body(step, _):
        q = q_ref[0]
        carry = carry_ref[...]
        total = None
        for u in range(unroll):
            s0 = pl.multiple_of(i * tq - (step * unroll + u + 1) * tk, tk)
            out, carry = _sb_tile(q, k_ref[0, pl.ds(s0, tk), :], v_ref[0, pl.ds(s0, tk), :], tri, carry, None)
            total = out if total is None else total + out
        acc_ref[...] += total
        carry_ref[...] = carry
        return 0

    lax.fori_loop(0, i * (nb // unroll), body, 0)
    o_ref[0] = acc_ref[...].astype(o_ref.dtype)


def _sb_attention(proj, tri, *, tq, tk):
    b, s, _ = proj.shape
    assert s % tq == 0 and tq % tk == 0 and tri.shape == (tk, tk)
    qoff, koff, voff = 0, N_SB_HEADS, 2 * N_SB_HEADS
    return pl.pallas_call(
        functools.partial(_sb_kernel, tq=tq, tk=tk, unroll=SB_UNROLL),
        out_shape=jax.ShapeDtypeStruct((b, s, SB_WIDTH), BF16),
        grid=(b, N_SB_HEADS, s // tq),
        in_specs=[
            pl.BlockSpec((1, tq, HEAD_DIM), lambda bi, h, i: (bi, i, qoff + h)),
            pl.BlockSpec((1, s, HEAD_DIM), lambda bi, h, i: (bi, 0, koff + h)),
            pl.BlockSpec((1, s, HEAD_DIM), lambda bi, h, i: (bi, 0, voff + h)),
            pl.BlockSpec((tk, tk), lambda bi, h, i: (0, 0)),
        ],
        out_specs=pl.BlockSpec((1, tq, HEAD_DIM), lambda bi, h, i: (bi, i, h)),
        scratch_shapes=[pltpu.VMEM((tq, HEAD_DIM), F32), pltpu.VMEM((tq, 1), F32)],
        compiler_params=pltpu.CompilerParams(
            dimension_semantics=("parallel", "parallel", "arbitrary"),
            vmem_limit_bytes=40 * MIB,
        ),
        name="stickbreak_attn",
    )(proj, proj, proj, tri)


def _diff_update(z, v, m_ref, l_ref, acc_ref, r0):
    m_old = m_ref[r0:, :]
    m_new = jnp.maximum(m_old, jnp.max(z, axis=-1, keepdims=True))
    alpha = jnp.exp2(m_old - m_new)
    p = jnp.exp2(z - m_new)
    l_ref[r0:, :] = alpha * l_ref[r0:, :] + jnp.sum(p, axis=-1, keepdims=True)
    acc_ref[r0:, :] = alpha * acc_ref[r0:, :] + _dot(p.astype(BF16), v)
    m_ref[r0:, :] = m_new


def _diff_kernel(q_ref, k_ref, v_ref, lq1_ref, lk1_ref, lq2_ref, lk2_ref, sub_ref, o_ref,
                 m1_ref, l1_ref, a1_ref, m2_ref, l2_ref, a2_ref, *, tq, tk, lambda_init):
    i = pl.program_id(2)
    nb = tq // tk

    m1_ref[...] = jnp.full_like(m1_ref, NEG_BIG)
    m2_ref[...] = jnp.full_like(m2_ref, NEG_BIG)
    l1_ref[...] = jnp.zeros_like(l1_ref)
    l2_ref[...] = jnp.zeros_like(l2_ref)
    a1_ref[...] = jnp.zeros_like(a1_ref)
    a2_ref[...] = jnp.zeros_like(a2_ref)

    def tile(s0, r0, width, mask):
        q = q_ref[0, r0:, :]
        k = k_ref[0, pl.ds(s0, width), :]
        v = v_ref[0, pl.ds(s0, width), :]
        z1 = _dot_nt(q[:, :HEAD_DIM], k[:, :HEAD_DIM])
        z2 = _dot_nt(q[:, HEAD_DIM:], k[:, HEAD_DIM:])
        if mask is not None:
            z1 = _mask_top_rows(z1, mask, NEG_BIG)
            z2 = _mask_top_rows(z2, mask, NEG_BIG)
        _diff_update(z1, v, m1_ref, l1_ref, a1_ref, r0)
        _diff_update(z2, v, m2_ref, l2_ref, a2_ref, r0)

    row = lax.broadcasted_iota(jnp.int32, (tk, tk), 0)
    col = lax.broadcasted_iota(jnp.int32, (tk, tk), 1)
    lower = col <= row
    for d in reversed(range(nb)):
        r0 = d * tk
        tile(pl.multiple_of(i * tq + r0, tk), r0, tk, lower)

    def body(step, _):
        tile(pl.multiple_of(step * tq, tq), 0, tq, None)
        return 0

    lax.fori_loop(0, i, body, 0)

    lam = (jnp.exp(jnp.sum(lq1_ref[...] * lk1_ref[...])) - jnp.exp(jnp.sum(lq2_ref[...] * lk2_ref[...]))
           + lambda_init)
    o = a1_ref[...] / l1_ref[...] - lam * (a2_ref[...] / l2_ref[...])
    o = _rmsnorm_rows(o, sub_ref[...]) * (1.0 - lambda_init)
    o_ref[0] = o.astype(o_ref.dtype)


def _diff_attention(proj, lq1, lk1, lq2, lk2, subln, lambda_init, *, tq, tk):
    b, s, _ = proj.shape
    assert s % tq == 0 and tq % tk == 0
    dv = 2 * HEAD_DIM
    qoff = 3 * SB_WIDTH // dv
    koff = qoff + N_DIFF_HEADS
    voff = koff + N_DIFF_HEADS
    vec = pl.BlockSpec((1, HEAD_DIM), lambda bi, h, i: (0, 0))
    return pl.pallas_call(
        functools.partial(_diff_kernel, tq=tq, tk=tk, lambda_init=lambda_init),
        out_shape=jax.ShapeDtypeStruct((b, s, DIFF_WIDTH), BF16),
        grid=(b, N_DIFF_HEADS, s // tq),
        in_specs=[
            pl.BlockSpec((1, tq, dv), lambda bi, h, i: (bi, i, qoff + h)),
            pl.BlockSpec((1, s, dv), lambda bi, h, i: (bi, 0, koff + h)),
            pl.BlockSpec((1, s, dv), lambda bi, h, i: (bi, 0, voff + h)),
            vec, vec, vec, vec,
            pl.BlockSpec((1, dv), lambda bi, h, i: (0, 0)),
        ],
        out_specs=pl.BlockSpec((1, tq, dv), lambda bi, h, i: (bi, i, h)),
        scratch_shapes=[
            pltpu.VMEM((tq, 1), F32), pltpu.VMEM((tq, 1), F32), pltpu.VMEM((tq, dv), F32),
            pltpu.VMEM((tq, 1), F32), pltpu.VMEM((tq, 1), F32), pltpu.VMEM((tq, dv), F32),
        ],
        compiler_params=pltpu.CompilerParams(
            dimension_semantics=("parallel", "parallel", "arbitrary"),
            vmem_limit_bytes=40 * MIB,
        ),
        name="diff_attn",
    )(proj, proj, proj, lq1, lk1, lq2, lk2, subln)


def _outproj_kernel(x_ref, sb_ref, df_ref, w1_ref, w2_ref, o_ref):
    o_ref[...] = x_ref[...] + _dot(sb_ref[...], w1_ref[...]) + _dot(df_ref[...], w2_ref[...])


def _outproj(x, sb, df, w_out, *, tm=1024, tn=1024):
    t, d = x.shape
    assert t % tm == 0 and d % tn == 0
    k1 = sb.shape[1]
    k2 = df.shape[1]
    assert k1 % tn == 0 or tn % k1 == 0
    return pl.pallas_call(
        _outproj_kernel,
        out_shape=jax.ShapeDtypeStruct((t, d), F32),
        grid=(t // tm, d // tn),
        in_specs=[
            pl.BlockSpec((tm, tn), lambda i, j: (i, j)),
            pl.BlockSpec((tm, k1), lambda i, j: (i, 0)),
            pl.BlockSpec((tm, k2), lambda i, j: (i, 0)),
            pl.BlockSpec((k1, tn), lambda i, j: (0, j)),
            pl.BlockSpec((k2, tn), lambda i, j: (1, j)),
        ],
        out_specs=pl.BlockSpec((tm, tn), lambda i, j: (i, j)),
        compiler_params=pltpu.CompilerParams(
            dimension_semantics=("parallel", "arbitrary"),
            vmem_limit_bytes=48 * MIB,
        ),
        name="outproj_residual",
    )(x, sb, df, w_out, w_out)


def _rope_tables(seq):
    pos = jnp.arange(seq, dtype=F32)
    inv_freq = ROPE_THETA ** (-jnp.arange(0, HEAD_DIM, 2, dtype=F32) / HEAD_DIM)
    ang = pos[:, None] * inv_freq[None, :]
    ang = jnp.concatenate([ang, ang], axis=-1)
    sign = jnp.concatenate([-jnp.ones((HEAD_DIM // 2,), F32), jnp.ones((HEAD_DIM // 2,), F32)])
    return jnp.cos(ang), jnp.sin(ang) * sign[None, :]


def kernel(x, norm_ffn1, w_ffn1_gate, w_ffn1_up, w_ffn1_down, norm_mix, w_in, lambda_q1, lambda_k1, lambda_q2, lambda_k2, diff_subln, w_out, norm_ffn2, w_ffn2_gate, w_ffn2_up, w_ffn2_down, norm_final):
    b, s, d = x.shape
    depth = w_in.shape[0]
    cos, sin = _rope_tables(s)
    tq, tk = 1024, 256
    tri = (lax.broadcasted_iota(jnp.int32, (tk, tk), 0) > lax.broadcasted_iota(jnp.int32, (tk, tk), 1)).astype(BF16)
    g_final = norm_final.reshape(1, d)

    xt = x.reshape(b * s, d)
    for layer in range(depth):
        lambda_init = 0.8 - 0.6 * math.exp(-0.3 * layer)
        xt = _ffn(xt, norm_ffn1[layer].reshape(1, d), w_ffn1_gate, w_ffn1_up, w_ffn1_down, g_final, layer,
                  final_norm=False)
        proj = _inproj(xt, norm_mix[layer].reshape(1, d), w_in[layer].astype(BF16), cos, sin, s)
        proj = proj.reshape(b, s, -1)
        sb = _sb_attention(proj, tri, tq=tq, tk=tk)
        df = _diff_attention(proj, lambda_q1[layer].reshape(1, -1), lambda_k1[layer].reshape(1, -1),
                             lambda_q2[layer].reshape(1, -1), lambda_k2[layer].reshape(1, -1),
                             diff_subln[layer].reshape(1, -1), lambda_init, tq=tq, tk=DIFF_BAND_KEYS)
        xt = _outproj(xt, sb.reshape(b * s, -1), df.reshape(b * s, -1), w_out[layer].astype(BF16))
        xt = _ffn(xt, norm_ffn2[layer].reshape(1, d), w_ffn2_gate, w_ffn2_up, w_ffn2_down, g_final, layer,
                  final_norm=(layer == depth - 1))
    return xt.reshape(b, s, d)
```

```python
import functools
import math

import jax
import jax.numpy as jnp
from jax import lax
from jax.experimental import pallas as pl
from jax.experimental.pallas import tpu as pltpu

HEAD_DIM = 128
N_SB_HEADS = 8
N_DIFF_HEADS = 4
SB_WIDTH = N_SB_HEADS * HEAD_DIM
DIFF_WIDTH = N_DIFF_HEADS * 2 * HEAD_DIM
ROPE_THETA = 10000.0
RMS_EPS = 1e-6
FFN_RESIDUAL_WEIGHT = 0.5
LOG2E = math.log2(math.e)
Q_SCALE = LOG2E / math.sqrt(HEAD_DIM)

SIDE_CAST_STEPS = 64
DIFF_HEADS_PER_STEP = 1
DIFF_BAND_KEYS = 1024
SB_UNROLL = 4
V7X_VMEM_BYTES = 64 * 1024 * 1024
MIB = 1024 * 1024
NEG_BIG = -1e30

F32 = jnp.float32
BF16 = jnp.bfloat16


def _rmsnorm_rows(x, g):
    ms = jnp.mean(x * x, axis=-1, keepdims=True)
    return x * lax.rsqrt(ms + RMS_EPS) * g


def _dot(a, b):
    return jnp.dot(a, b, preferred_element_type=F32)


def _dot_nt(a, b):
    return lax.dot_general(a, b, (((1,), (1,)), ((), ())), preferred_element_type=F32)


def _ffn_kernel(x_ref, g_ref, wg_ref, wu_ref, wd_ref, gf_ref, *rest, final_norm, emit_bf16, n_side, side_steps):
    side_in = rest[:n_side]
    o_ref = rest[n_side]
    w16_refs = rest[n_side + 1:n_side + 4] if emit_bf16 else ()
    side_out = rest[len(rest) - 1 - n_side:len(rest) - 1]
    h_ref = rest[-1]
    i = pl.program_id(0)
    j = pl.program_id(1)
    nj = pl.num_programs(1)

    @pl.when(j == 0)
    def _():
        x = x_ref[...]
        h_ref[...] = _rmsnorm_rows(x, g_ref[...]).astype(BF16)
        o_ref[...] = x

    if n_side:
        @pl.when(i * nj + j < side_steps)
        def _():
            for src, dst in zip(side_in, side_out):
                dst[...] = src[...].astype(BF16)

    wg = wg_ref[...].astype(BF16)
    wu = wu_ref[...].astype(BF16)
    wd = wd_ref[...].astype(BF16)
    if emit_bf16:
        wg16_ref, wu16_ref, wd16_ref = w16_refs
        wg16_ref[...] = wg
        wu16_ref[...] = wu
        wd16_ref[...] = wd

    h = h_ref[...]
    gate = _dot(h, wg)
    up = _dot(h, wu)
    act = (gate / (1.0 + jnp.exp(-gate))) * (up * FFN_RESIDUAL_WEIGHT)
    o_ref[...] += _dot(act.astype(BF16), wd)

    if final_norm:
        @pl.when(j == nj - 1)
        def _():
            o_ref[...] = _rmsnorm_rows(o_ref[...], gf_ref[...])


def _ffn_call(x, g, wg, wu, wd, g_final, *, layer, final_norm, row_block0, n_row_blocks, tm, tf,
              side=(), side_layer=None):
    t, d = x.shape
    emit_bf16 = layer is not None
    d_ff = wg.shape[-1]
    nj = d_ff // tf
    assert d_ff % tf == 0 and (row_block0 + n_row_blocks) * tm <= t
    if emit_bf16:
        w_in_specs = [
            pl.BlockSpec((None, d, tf), lambda i, j: (layer, 0, j)),
            pl.BlockSpec((None, d, tf), lambda i, j: (layer, 0, j)),
            pl.BlockSpec((None, tf, d), lambda i, j: (layer, j, 0)),
        ]
    else:
        w_in_specs = [
            pl.BlockSpec((d, tf), lambda i, j: (0, j)),
            pl.BlockSpec((d, tf), lambda i, j: (0, j)),
            pl.BlockSpec((tf, d), lambda i, j: (j, 0)),
        ]
    out_shape = [jax.ShapeDtypeStruct((t, d), F32)]
    out_specs = [pl.BlockSpec((tm, d), lambda i, j: (i + row_block0, 0))]
    if emit_bf16:
        out_shape += [jax.ShapeDtypeStruct((d, d_ff), BF16), jax.ShapeDtypeStruct((d, d_ff), BF16),
                      jax.ShapeDtypeStruct((d_ff, d), BF16)]
        out_specs += [pl.BlockSpec((d, tf), lambda i, j: (0, j)), pl.BlockSpec((d, tf), lambda i, j: (0, j)),
                      pl.BlockSpec((tf, d), lambda i, j: (j, 0))]
    side_steps = SIDE_CAST_STEPS if side else 0
    assert side_steps <= n_row_blocks * nj
    side_in_specs = []
    for arr in side:
        _, rows, cols = arr.shape
        assert rows % side_steps == 0
        slab = rows // side_steps
        side_in_specs.append(pl.BlockSpec(
            (None, slab, cols), lambda i, j: (side_layer, jnp.minimum(i * nj + j, side_steps - 1), 0)))
        out_shape.append(jax.ShapeDtypeStruct((rows, cols), BF16))
        out_specs.append(pl.BlockSpec((slab, cols), lambda i, j: (jnp.minimum(i * nj + j, side_steps - 1), 0)))
    x_mode = dict(pipeline_mode=pl.Buffered(1)) if n_row_blocks == 1 else {}
    res = pl.pallas_call(
        functools.partial(_ffn_kernel, final_norm=final_norm, emit_bf16=emit_bf16, n_side=len(side),
                          side_steps=side_steps),
        out_shape=out_shape,
        grid=(n_row_blocks, nj),
        in_specs=[
            pl.BlockSpec((tm, d), lambda i, j: (i + row_block0, 0), **x_mode),
            pl.BlockSpec((1, d), lambda i, j: (0, 0)),
            *w_in_specs,
            pl.BlockSpec((1, d), lambda i, j: (0, 0)),
            *side_in_specs,
        ],
        out_specs=out_specs,
        scratch_shapes=[pltpu.VMEM((tm, d), BF16)],
        input_output_aliases={0: 0},
        compiler_params=pltpu.CompilerParams(
            dimension_semantics=("arbitrary", "arbitrary"),
            vmem_limit_bytes=58 * MIB,
        ),
        name="ffn_swiglu_cast" if emit_bf16 else "ffn_swiglu",
    )(x, g, wg, wu, wd, g_final, *side)
    return res


def _ffn(x, g, wg32, wu32, wd32, g_final, layer, *, final_norm, side=(), tm=1024, tf=512, tf_first=256):
    t, _ = x.shape
    x, wg, wu, wd = _ffn_call(x, g, wg32, wu32, wd32, g_final, layer=layer, final_norm=final_norm,
                              row_block0=0, n_row_blocks=1, tm=tm, tf=tf_first)
    x, *side16 = _ffn_call(x, g, wg, wu, wd, g_final, layer=None, final_norm=final_norm,
                           row_block0=1, n_row_blocks=t // tm - 1, tm=tm, tf=tf, side=side, side_layer=layer)
    return x, side16


def _inproj_kernel(x_ref, g_ref, w_ref, cos_ref, sin_ref, o_ref, h_ref, *, tn):
    j = pl.program_id(1)

    @pl.when(j == 0)
    def _():
        h_ref[...] = _rmsnorm_rows(x_ref[...], g_ref[...]).astype(BF16)

    acc = _dot(h_ref[...], w_ref[...])
    col = j * tn
    sbq_end = SB_WIDTH
    dq_start = 3 * SB_WIDTH
    dk_start = dq_start + DIFF_WIDTH
    dv_start = dk_start + DIFF_WIDTH

    def rope(y):
        cos = cos_ref[...]
        sin = sin_ref[...]
        parts = []
        for c in range(tn // HEAD_DIM):
            yc = y[:, c * HEAD_DIM:(c + 1) * HEAD_DIM]
            parts.append(yc * cos + pltpu.roll(yc, HEAD_DIM // 2, 1) * sin)
        return jnp.concatenate(parts, axis=1)

    @pl.when(col < sbq_end)
    def _():
        o_ref[...] = (acc * Q_SCALE).astype(o_ref.dtype)

    @pl.when(jnp.logical_or(jnp.logical_and(col >= sbq_end, col < dq_start), col >= dv_start))
    def _():
        o_ref[...] = acc.astype(o_ref.dtype)

    @pl.when(jnp.logical_and(col >= dq_start, col < dk_start))
    def _():
        o_ref[...] = (rope(acc) * Q_SCALE).astype(o_ref.dtype)

    @pl.when(jnp.logical_and(col >= dk_start, col < dv_start))
    def _():
        o_ref[...] = rope(acc).astype(o_ref.dtype)


def _inproj(x, g, w, cos, sin, seq, *, tm=1024, tn=1024):
    t, d = x.shape
    d_in = w.shape[1]
    assert t % tm == 0 and d_in % tn == 0 and seq % tm == 0 and SB_WIDTH % tn == 0
    pos_blocks = seq // tm
    return pl.pallas_call(
        functools.partial(_inproj_kernel, tn=tn),
        out_shape=jax.ShapeDtypeStruct((t, d_in), BF16),
        grid=(t // tm, d_in // tn),
        in_specs=[
            pl.BlockSpec((tm, d), lambda i, j: (i, 0)),
            pl.BlockSpec((1, d), lambda i, j: (0, 0)),
            pl.BlockSpec((d, tn), lambda i, j: (0, j)),
            pl.BlockSpec((tm, HEAD_DIM), lambda i, j: (i % pos_blocks, 0)),
            pl.BlockSpec((tm, HEAD_DIM), lambda i, j: (i % pos_blocks, 0)),
        ],
        out_specs=pl.BlockSpec((tm, tn), lambda i, j: (i, j)),
        scratch_shapes=[pltpu.VMEM((tm, d), BF16)],
        compiler_params=pltpu.CompilerParams(
            dimension_semantics=("parallel", "arbitrary"),
            vmem_limit_bytes=48 * MIB,
        ),
        name="inproj_rope",
    )(x, g, w, cos, sin)


def _neg_abs(z):
    return pltpu.bitcast(pltpu.bitcast(z, jnp.uint32) | jnp.uint32(0x80000000), F32)


def _mask_top_rows(x, mask, fill):
    n = mask.shape[0]
    top = jnp.where(mask, x[:n], fill)
    return top if n == x.shape[0] else jnp.concatenate([top, x[n:]], axis=0)


def _sb_tile(q, k, v, tri, carry, mask):
    z = _dot_nt(q, k)
    sp = jnp.maximum(z, 0.0) + jnp.log(1.0 + jnp.exp2(_neg_abs(z))) * LOG2E
    if mask is not None:
        sp = _mask_top_rows(sp, mask, 0.0)
    excl = _dot(sp.astype(BF16), tri)
    w = jnp.exp2((z - sp) - excl - carry)
    if mask is not None:
        w = _mask_top_rows(w, mask, 0.0)
    out = _dot(w.astype(BF16), v)
    return out, carry + (excl[:, 0:1] + sp[:, 0:1])


def _sb_kernel(q_ref, k_ref, v_ref, tri_ref, o_ref, acc_ref, carry_ref, *, tq, tk, unroll):
    i = pl.program_id(2)
    nb = tq // tk
    assert nb % unroll == 0
    tri = tri_ref[...]

    row = lax.broadcasted_iota(jnp.int32, (tk, tk), 0)
    col = lax.broadcasted_iota(jnp.int32, (tk, tk), 1)
    strict_lower = col < row
    acc = None
    carry = None
    for d in reversed(range(nb)):
        r0 = d * tk
        s0 = pl.multiple_of(i * tq + r0, tk)
        zero_c = jnp.zeros((tk, 1), F32)
        carry_in = zero_c if carry is None else jnp.concatenate([zero_c, carry], axis=0)
        out, carry = _sb_tile(q_ref[0, r0:, :], k_ref[0, pl.ds(s0, tk), :], v_ref[0, pl.ds(s0, tk), :], tri,
                              carry_in, strict_lower)
        acc = out if acc is None else out + jnp.concatenate([jnp.zeros((tk, HEAD_DIM), F32), acc], axis=0)
    acc_ref[...] = acc
    carry_ref[...] = carry

    def body(step, _):
        q = q_ref[0]
        carry = carry_ref[...]
        total = None
        for u in range(unroll):
            s0 = pl.multiple_of(i * tq - (step * unroll + u + 1) * tk, tk)
            out, carry = _sb_tile(q, k_ref[0, pl.ds(s0, tk), :], v_ref[0, pl.ds(s0, tk), :], tri, carry, None)
            total = out if total is None else total + out
        acc_ref[...] += total
        carry_ref[...] = carry
        return 0

    lax.fori_loop(0, i * (nb // unroll), body, 0)
    o_ref[0] = acc_ref[...].astype(o_ref.dtype)


def _sb_attention(proj, tri, *, tq, tk):
    b, s, _ = proj.shape
    assert s % tq == 0 and tq % tk == 0 and tri.shape == (tk, tk)
    qoff, koff, voff = 0, N_SB_HEADS, 2 * N_SB_HEADS
    return pl.pallas_call(
        functools.partial(_sb_kernel, tq=tq, tk=tk, unroll=SB_UNROLL),
        out_shape=jax.ShapeDtypeStruct((b, s, SB_WIDTH), BF16),
        grid=(b, N_SB_HEADS, s // tq),
        in_specs=[
            pl.BlockSpec((1, tq, HEAD_DIM), lambda bi, h, i: (bi, i, qoff + h)),
            pl.BlockSpec((1, s, HEAD_DIM), lambda bi, h, i: (bi, 0, koff + h)),
            pl.BlockSpec((1, s, HEAD_DIM), lambda bi, h, i: (bi, 0, voff + h)),
            pl.BlockSpec((tk, tk), lambda bi, h, i: (0, 0)),
        ],
        out_specs=pl.BlockSpec((1, tq, HEAD_DIM), lambda bi, h, i: (bi, i, h)),
        scratch_shapes=[pltpu.VMEM((tq, HEAD_DIM), F32), pltpu.VMEM((tq, 1), F32)],
        compiler_params=pltpu.CompilerParams(
            dimension_semantics=("parallel", "parallel", "arbitrary"),
            vmem_limit_bytes=40 * MIB,
        ),
        name="stickbreak_attn",
    )(proj, proj, proj, tri)


def _diff_update(z, v, m_ref, l_ref, acc_ref, c, r0):
    m_old = m_ref[c, r0:, :]
    m_new = jnp.maximum(m_old, jnp.max(z, axis=-1, keepdims=True))
    alpha = jnp.exp2(m_old - m_new)
    p = jnp.exp2(z - m_new)
    l_ref[c, r0:, :] = alpha * l_ref[c, r0:, :] + jnp.sum(p, axis=-1, keepdims=True)
    acc_ref[c, r0:, :] = alpha * acc_ref[c, r0:, :] + _dot(p.astype(BF16), v)
    m_ref[c, r0:, :] = m_new


def _diff_kernel(q_ref, k_ref, v_ref, lq1_ref, lk1_ref, lq2_ref, lk2_ref, sub_ref, o_ref,
                 m_ref, l_ref, acc_ref, *, tq, tk, heads, lambda_init):
    i = pl.program_id(2)
    nb = tq // tk
    dv = 2 * HEAD_DIM

    m_ref[...] = jnp.full_like(m_ref, NEG_BIG)
    l_ref[...] = jnp.zeros_like(l_ref)
    acc_ref[...] = jnp.zeros_like(acc_ref)

    def tile(s0, r0, width, mask):
        zs = []
        for e in range(heads):
            for mp in range(2):
                c0 = e * dv + mp * HEAD_DIM
                z = _dot_nt(q_ref[0, r0:, c0:c0 + HEAD_DIM], k_ref[0, pl.ds(s0, width), c0:c0 + HEAD_DIM])
                if mask is not None:
                    z = _mask_top_rows(z, mask, NEG_BIG)
                zs.append(z)
        for e in range(heads):
            v = v_ref[0, pl.ds(s0, width), e * dv:(e + 1) * dv]
            for mp in range(2):
                _diff_update(zs[2 * e + mp], v, m_ref, l_ref, acc_ref, 2 * e + mp, r0)

    row = lax.broadcasted_iota(jnp.int32, (tk, tk), 0)
    col = lax.broadcasted_iota(jnp.int32, (tk, tk), 1)
    lower = col <= row
    for d in reversed(range(nb)):
        r0 = d * tk
        tile(pl.multiple_of(i * tq + r0, tk), r0, tk, lower)

    def body(step, _):
        tile(pl.multiple_of(step * tq, tq), 0, tq, None)
        return 0

    lax.fori_loop(0, i, body, 0)

    lam = (jnp.exp(jnp.sum(lq1_ref[...] * lk1_ref[...])) - jnp.exp(jnp.sum(lq2_ref[...] * lk2_ref[...]))
           + lambda_init)
    for e in range(heads):
        o = acc_ref[2 * e] / l_ref[2 * e] - lam * (acc_ref[2 * e + 1] / l_ref[2 * e + 1])
        o = _rmsnorm_rows(o, sub_ref[...]) * (1.0 - lambda_init)
        o_ref[0, :, e * dv:(e + 1) * dv] = o.astype(o_ref.dtype)


def _diff_attention(proj, lq1, lk1, lq2, lk2, subln, lambda_init, *, tq, tk, heads=DIFF_HEADS_PER_STEP):
    b, s, _ = proj.shape
    assert s % tq == 0 and tq % tk == 0 and N_DIFF_HEADS % heads == 0
    dv = 2 * HEAD_DIM
    w = heads * dv
    qoff = 3 * SB_WIDTH // w
    koff = qoff + N_DIFF_HEADS // heads
    voff = koff + N_DIFF_HEADS // heads
    vec = pl.BlockSpec((1, HEAD_DIM), lambda bi, h, i: (0, 0))
    return pl.pallas_call(
        functools.partial(_diff_kernel, tq=tq, tk=tk, heads=heads, lambda_init=lambda_init),
        out_shape=jax.ShapeDtypeStruct((b, s, DIFF_WIDTH), BF16),
        grid=(b, N_DIFF_HEADS // heads, s // tq),
        in_specs=[
            pl.BlockSpec((1, tq, w), lambda bi, h, i: (bi, i, qoff + h)),
            pl.BlockSpec((1, s, w), lambda bi, h, i: (bi, 0, koff + h)),
            pl.BlockSpec((1, s, w), lambda bi, h, i: (bi, 0, voff + h)),
            vec, vec, vec, vec,
            pl.BlockSpec((1, dv), lambda bi, h, i: (0, 0)),
        ],
        out_specs=pl.BlockSpec((1, tq, w), lambda bi, h, i: (bi, i, h)),
        scratch_shapes=[
            pltpu.VMEM((2 * heads, tq, 1), F32), pltpu.VMEM((2 * heads, tq, 1), F32),
            pltpu.VMEM((2 * heads, tq, dv), F32),
        ],
        compiler_params=pltpu.CompilerParams(
            dimension_semantics=("parallel", "parallel", "arbitrary"),
            vmem_limit_bytes=56 * MIB,
        ),
        name="diff_attn",
    )(proj, proj, proj, lq1, lk1, lq2, lk2, subln)


def _outproj_kernel(x_ref, sb_ref, df_ref, w1_ref, w2_ref, o_ref):
    o_ref[...] = x_ref[...] + _dot(sb_ref[...], w1_ref[...]) + _dot(df_ref[...], w2_ref[...])


def _outproj(x, sb, df, w_out, *, tm=1024, tn=1024):
    t, d = x.shape
    assert t % tm == 0 and d % tn == 0
    k1 = sb.shape[1]
    k2 = df.shape[1]
    assert k1 % tn == 0 or tn % k1 == 0
    return pl.pallas_call(
        _outproj_kernel,
        out_shape=jax.ShapeDtypeStruct((t, d), F32),
        grid=(t // tm, d // tn),
        in_specs=[
            pl.BlockSpec((tm, tn), lambda i, j: (i, j)),
            pl.BlockSpec((tm, k1), lambda i, j: (i, 0)),
            pl.BlockSpec((tm, k2), lambda i, j: (i, 0)),
            pl.BlockSpec((k1, tn), lambda i, j: (0, j)),
            pl.BlockSpec((k2, tn), lambda i, j: (1, j)),
        ],
        out_specs=pl.BlockSpec((tm, tn), lambda i, j: (i, j)),
        compiler_params=pltpu.CompilerParams(
            dimension_semantics=("parallel", "arbitrary"),
            vmem_limit_bytes=48 * MIB,
        ),
        name="outproj_residual",
    )(x, sb, df, w_out, w_out)


def _rope_tables(seq):
    pos = jnp.arange(seq, dtype=F32)
    inv_freq = ROPE_THETA ** (-jnp.arange(0, HEAD_DIM, 2, dtype=F32) / HEAD_DIM)
    ang = pos[:, None] * inv_freq[None, :]
    ang = jnp.concatenate([ang, ang], axis=-1)
    sign = jnp.concatenate([-jnp.ones((HEAD_DIM // 2,), F32), jnp.ones((HEAD_DIM // 2,), F32)])
    return jnp.cos(ang), jnp.sin(ang) * sign[None, :]


def kernel(x, norm_ffn1, w_ffn1_gate, w_ffn1_up, w_ffn1_down, norm_mix, w_in, lambda_q1, lambda_k1, lambda_q2, lambda_k2, diff_subln, w_out, norm_ffn2, w_ffn2_gate, w_ffn2_up, w_ffn2_down, norm_final):
    b, s, d = x.shape
    depth = w_in.shape[0]
    cos, sin = _rope_tables(s)
    tq, tk = 1024, 256
    tri = (lax.broadcasted_iota(jnp.int32, (tk, tk), 0) > lax.broadcasted_iota(jnp.int32, (tk, tk), 1)).astype(BF16)
    g_final = norm_final.reshape(1, d)

    xt = x.reshape(b * s, d)
    for layer in range(depth):
        lambda_init = 0.8 - 0.6 * math.exp(-0.3 * layer)
        xt, (w_in16, w_out16) = _ffn(xt, norm_ffn1[layer].reshape(1, d), w_ffn1_gate, w_ffn1_up, w_ffn1_down,
                                     g_final, layer, final_norm=False, side=(w_in, w_out))
        proj = _inproj(xt, norm_mix[layer].reshape(1, d), w_in16, cos, sin, s)
        proj = proj.reshape(b, s, -1)
        sb = _sb_attention(proj, tri, tq=tq, tk=tk)
        df = _diff_attention(proj, lambda_q1[layer].reshape(1, -1), lambda_k1[layer].reshape(1, -1),
                             lambda_q2[layer].reshape(1, -1), lambda_k2[layer].reshape(1, -1),
                             diff_subln[layer].reshape(1, -1), lambda_init, tq=tq, tk=DIFF_BAND_KEYS)
        xt = _outproj(xt, sb.reshape(b * s, -1), df.reshape(b * s, -1), w_out16)
        xt, _ = _ffn(xt, norm_ffn2[layer].reshape(1, d), w_ffn2_gate, w_ffn2_up, w_ffn2_down, g_final, layer,
                     final_norm=(layer == depth - 1))
    return xt.reshape(b, s, d)
```

```python
import functools
import math

import jax
import jax.numpy as jnp
from jax import lax
from jax.experimental import pallas as pl
from jax.experimental.pallas import tpu as pltpu

HEAD_DIM = 128
N_SB_HEADS = 8
N_DIFF_HEADS = 4
SB_WIDTH = N_SB_HEADS * HEAD_DIM
DIFF_WIDTH = N_DIFF_HEADS * 2 * HEAD_DIM
ROPE_THETA = 10000.0
RMS_EPS = 1e-6
FFN_RESIDUAL_WEIGHT = 0.5
LOG2E = math.log2(math.e)
Q_SCALE = LOG2E / math.sqrt(HEAD_DIM)

SIDE_CAST_STEPS = 64
SOFTPLUS2_LINEAR = 64.0
SB_UNROLL = 4
V7X_VMEM_BYTES = 64 * 1024 * 1024
MIB = 1024 * 1024
NEG_BIG = -1e30

F32 = jnp.float32
BF16 = jnp.bfloat16


def _rmsnorm_rows(x, g):
    ms = jnp.mean(x * x, axis=-1, keepdims=True)
    return x * lax.rsqrt(ms + RMS_EPS) * g


def _dot(a, b):
    return jnp.dot(a, b, preferred_element_type=F32)


def _dot_nt(a, b):
    return lax.dot_general(a, b, (((1,), (1,)), ((), ())), preferred_element_type=F32)


def _ffn_kernel(x_ref, g_ref, wg_ref, wu_ref, wd_ref, gf_ref, *rest, final_norm, emit_bf16, n_side, side_steps):
    side_in = rest[:n_side]
    o_ref = rest[n_side]
    w16_refs = rest[n_side + 1:n_side + 4] if emit_bf16 else ()
    side_out = rest[len(rest) - 1 - n_side:len(rest) - 1]
    h_ref = rest[-1]
    i = pl.program_id(0)
    j = pl.program_id(1)
    nj = pl.num_programs(1)

    @pl.when(j == 0)
    def _():
        x = x_ref[...]
        h_ref[...] = _rmsnorm_rows(x, g_ref[...]).astype(BF16)
        o_ref[...] = x

    if n_side:
        @pl.when(i * nj + j < side_steps)
        def _():
            for src, dst in zip(side_in, side_out):
                dst[...] = src[...].astype(BF16)

    wg = wg_ref[...].astype(BF16)
    wu = wu_ref[...].astype(BF16)
    wd = wd_ref[...].astype(BF16)
    if emit_bf16:
        wg16_ref, wu16_ref, wd16_ref = w16_refs
        wg16_ref[...] = wg
        wu16_ref[...] = wu
        wd16_ref[...] = wd

    h = h_ref[...]
    gate = _dot(h, wg)
    up = _dot(h, wu)
    act = (gate / (1.0 + jnp.exp(-gate))) * (up * FFN_RESIDUAL_WEIGHT)
    o_ref[...] += _dot(act.astype(BF16), wd)

    if final_norm:
        @pl.when(j == nj - 1)
        def _():
            o_ref[...] = _rmsnorm_rows(o_ref[...], gf_ref[...])


def _ffn_call(x, g, wg, wu, wd, g_final, *, layer, final_norm, row_block0, n_row_blocks, tm, tf,
              side=(), side_layer=None):
    t, d = x.shape
    emit_bf16 = layer is not None
    d_ff = wg.shape[-1]
    nj = d_ff // tf
    assert d_ff % tf == 0 and (row_block0 + n_row_blocks) * tm <= t
    if emit_bf16:
        w_in_specs = [
            pl.BlockSpec((None, d, tf), lambda i, j: (layer, 0, j)),
            pl.BlockSpec((None, d, tf), lambda i, j: (layer, 0, j)),
            pl.BlockSpec((None, tf, d), lambda i, j: (layer, j, 0)),
        ]
    else:
        w_in_specs = [
            pl.BlockSpec((d, tf), lambda i, j: (0, j)),
            pl.BlockSpec((d, tf), lambda i, j: (0, j)),
            pl.BlockSpec((tf, d), lambda i, j: (j, 0)),
        ]
    out_shape = [jax.ShapeDtypeStruct((t, d), F32)]
    out_specs = [pl.BlockSpec((tm, d), lambda i, j: (i + row_block0, 0))]
    if emit_bf16:
        out_shape += [jax.ShapeDtypeStruct((d, d_ff), BF16), jax.ShapeDtypeStruct((d, d_ff), BF16),
                      jax.ShapeDtypeStruct((d_ff, d), BF16)]
        out_specs += [pl.BlockSpec((d, tf), lambda i, j: (0, j)), pl.BlockSpec((d, tf), lambda i, j: (0, j)),
                      pl.BlockSpec((tf, d), lambda i, j: (j, 0))]
    side_steps = SIDE_CAST_STEPS if side else 0
    assert side_steps <= n_row_blocks * nj
    side_in_specs = []
    for arr in side:
        _, rows, cols = arr.shape
        assert rows % side_steps == 0
        slab = rows // side_steps
        side_in_specs.append(pl.BlockSpec(
            (None, slab, cols), lambda i, j: (side_layer, jnp.minimum(i * nj + j, side_steps - 1), 0)))
        out_shape.append(jax.ShapeDtypeStruct((rows, cols), BF16))
        out_specs.append(pl.BlockSpec((slab, cols), lambda i, j: (jnp.minimum(i * nj + j, side_steps - 1), 0)))
    x_mode = dict(pipeline_mode=pl.Buffered(1)) if n_row_blocks == 1 else {}
    res = pl.pallas_call(
        functools.partial(_ffn_kernel, final_norm=final_norm, emit_bf16=emit_bf16, n_side=len(side),
                          side_steps=side_steps),
        out_shape=out_shape,
        grid=(n_row_blocks, nj),
        in_specs=[
            pl.BlockSpec((tm, d), lambda i, j: (i + row_block0, 0), **x_mode),
            pl.BlockSpec((1, d), lambda i, j: (0, 0)),
            *w_in_specs,
            pl.BlockSpec((1, d), lambda i, j: (0, 0)),
            *side_in_specs,
        ],
        out_specs=out_specs,
        scratch_shapes=[pltpu.VMEM((tm, d), BF16)],
        input_output_aliases={0: 0},
        compiler_params=pltpu.CompilerParams(
            dimension_semantics=("arbitrary", "arbitrary"),
            vmem_limit_bytes=58 * MIB,
        ),
        name="ffn_swiglu_cast" if emit_bf16 else "ffn_swiglu",
    )(x, g, wg, wu, wd, g_final, *side)
    return res


def _ffn(x, g, wg32, wu32, wd32, g_final, layer, *, final_norm, side=(), tm=1024, tf=512, tf_first=256):
    t, _ = x.shape
    x, wg, wu, wd = _ffn_call(x, g, wg32, wu32, wd32, g_final, layer=layer, final_norm=final_norm,
                              row_block0=0, n_row_blocks=1, tm=tm, tf=tf_first)
    x, *side16 = _ffn_call(x, g, wg, wu, wd, g_final, layer=None, final_norm=final_norm,
                           row_block0=1, n_row_blocks=t // tm - 1, tm=tm, tf=tf, side=side, side_layer=layer)
    return x, side16


def _inproj_kernel(x_ref, g_ref, w_ref, cos_ref, sin_ref, o_ref, h_ref, *, tn):
    j = pl.program_id(1)

    @pl.when(j == 0)
    def _():
        h_ref[...] = _rmsnorm_rows(x_ref[...], g_ref[...]).astype(BF16)

    acc = _dot(h_ref[...], w_ref[...])
    col = j * tn
    sbq_end = SB_WIDTH
    dq_start = 3 * SB_WIDTH
    dk_start = dq_start + DIFF_WIDTH
    dv_start = dk_start + DIFF_WIDTH

    def rope(y):
        cos = cos_ref[...]
        sin = sin_ref[...]
        parts = []
        for c in range(tn // HEAD_DIM):
            yc = y[:, c * HEAD_DIM:(c + 1) * HEAD_DIM]
            parts.append(yc * cos + pltpu.roll(yc, HEAD_DIM // 2, 1) * sin)
        return jnp.concatenate(parts, axis=1)

    @pl.when(col < sbq_end)
    def _():
        o_ref[...] = (acc * Q_SCALE).astype(o_ref.dtype)

    @pl.when(jnp.logical_or(jnp.logical_and(col >= sbq_end, col < dq_start), col >= dv_start))
    def _():
        o_ref[...] = acc.astype(o_ref.dtype)

    @pl.when(jnp.logical_and(col >= dq_start, col < dk_start))
    def _():
        o_ref[...] = (rope(acc) * Q_SCALE).astype(o_ref.dtype)

    @pl.when(jnp.logical_and(col >= dk_start, col < dv_start))
    def _():
        o_ref[...] = rope(acc).astype(o_ref.dtype)


def _inproj(x, g, w, cos, sin, seq, *, tm=1024, tn=1024):
    t, d = x.shape
    d_in = w.shape[1]
    assert t % tm == 0 and d_in % tn == 0 and seq % tm == 0 and SB_WIDTH % tn == 0
    pos_blocks = seq // tm
    return pl.pallas_call(
        functools.partial(_inproj_kernel, tn=tn),
        out_shape=jax.ShapeDtypeStruct((t, d_in), BF16),
        grid=(t // tm, d_in // tn),
        in_specs=[
            pl.BlockSpec((tm, d), lambda i, j: (i, 0)),
            pl.BlockSpec((1, d), lambda i, j: (0, 0)),
            pl.BlockSpec((d, tn), lambda i, j: (0, j)),
            pl.BlockSpec((tm, HEAD_DIM), lambda i, j: (i % pos_blocks, 0)),
            pl.BlockSpec((tm, HEAD_DIM), lambda i, j: (i % pos_blocks, 0)),
        ],
        out_specs=pl.BlockSpec((tm, tn), lambda i, j: (i, j)),
        scratch_shapes=[pltpu.VMEM((tm, d), BF16)],
        compiler_params=pltpu.CompilerParams(
            dimension_semantics=("parallel", "arbitrary"),
            vmem_limit_bytes=48 * MIB,
        ),
        name="inproj_rope",
    )(x, g, w, cos, sin)


def _mask_top_rows(x, mask, fill):
    n = mask.shape[0]
    top = jnp.where(mask, x[:n], fill)
    return top if n == x.shape[0] else jnp.concatenate([top, x[n:]], axis=0)


def _sb_tile(q, k, v, tri, carry, mask):
    z = _dot_nt(q, k)
    sp = jnp.where(z > SOFTPLUS2_LINEAR, z, jnp.log(1.0 + jnp.exp2(z)) * LOG2E)
    if mask is not None:
        sp = _mask_top_rows(sp, mask, 0.0)
    excl = _dot(sp.astype(BF16), tri)
    w = jnp.exp2((z - sp) - excl - carry)
    if mask is not None:
        w = _mask_top_rows(w, mask, 0.0)
    out = _dot(w.astype(BF16), v)
    return out, carry + (excl[:, 0:1] + sp[:, 0:1])


def _sb_kernel(q_ref, k_ref, v_ref, tri_ref, o_ref, acc_ref, carry_ref, *, tq, tk, unroll):
    i = pl.program_id(2)
    nb = tq // tk
    assert nb % unroll == 0
    tri = tri_ref[...]

    row = lax.broadcasted_iota(jnp.int32, (tk, tk), 0)
    col = lax.broadcasted_iota(jnp.int32, (tk, tk), 1)
    strict_lower = col < row
    acc = None
    carry = None
    for d in reversed(range(nb)):
        r0 = d * tk
        s0 = pl.multiple_of(i * tq + r0, tk)
        zero_c = jnp.zeros((tk, 1), F32)
        carry_in = zero_c if carry is None else jnp.concatenate([zero_c, carry], axis=0)
        out, carry = _sb_tile(q_ref[0, r0:, :], k_ref[0, pl.ds(s0, tk), :], v_ref[0, pl.ds(s0, tk), :], tri,
                              carry_in, strict_lower)
        acc = out if acc is None else out + jnp.concatenate([jnp.zeros((tk, HEAD_DIM), F32), acc], axis=0)
    acc_ref[...] = acc
    carry_ref[...] = carry

    def body(step, _):
        q = q_ref[0]
        carry = carry_ref[...]
        total = None
        for u in range(unroll):
            s0 = pl.multiple_of(i * tq - (step * unroll + u + 1) * tk, tk)
            out, carry = _sb_tile(q, k_ref[0, pl.ds(s0, tk), :], v_ref[0, pl.ds(s0, tk), :], tri, carry, None)
            total = out if total is None else total + out
        acc_ref[...] += total
        carry_ref[...] = carry
        return 0

    lax.fori_loop(0, i * (nb // unroll), body, 0)
    o_ref[0] = acc_ref[...].astype(o_ref.dtype)


def _sb_attention(proj, tri, *, tq, tk):
    b, s, _ = proj.shape
    assert s % tq == 0 and tq % tk == 0 and tri.shape == (tk, tk)
    qoff, koff, voff = 0, N_SB_HEADS, 2 * N_SB_HEADS
    return pl.pallas_call(
        functools.partial(_sb_kernel, tq=tq, tk=tk, unroll=SB_UNROLL),
        out_shape=jax.ShapeDtypeStruct((b, s, SB_WIDTH), BF16),
        grid=(b, N_SB_HEADS, s // tq),
        in_specs=[
            pl.BlockSpec((1, tq, HEAD_DIM), lambda bi, h, i: (bi, i, qoff + h)),
            pl.BlockSpec((1, s, HEAD_DIM), lambda bi, h, i: (bi, 0, koff + h)),
            pl.BlockSpec((1, s, HEAD_DIM), lambda bi, h, i: (bi, 0, voff + h)),
            pl.BlockSpec((tk, tk), lambda bi, h, i: (0, 0)),
        ],
        out_specs=pl.BlockSpec((1, tq, HEAD_DIM), lambda bi, h, i: (bi, i, h)),
        scratch_shapes=[pltpu.VMEM((tq, HEAD_DIM), F32), pltpu.VMEM((tq, 1), F32)],
        compiler_params=pltpu.CompilerParams(
            dimension_semantics=("parallel", "parallel", "arbitrary"),
            vmem_limit_bytes=40 * MIB,
        ),
        name="stickbreak_attn",
    )(proj, proj, proj, tri)


def _diff_kernel(q_ref, k_ref, v_ref, lq1_ref, lk1_ref, lq2_ref, lk2_ref, sub_ref, o_ref,
                 m_ref, l_ref, acc_ref, *, tq, lambda_init):
    i = pl.program_id(2)

    m_ref[...] = jnp.full_like(m_ref, NEG_BIG)
    l_ref[...] = jnp.zeros_like(l_ref)
    acc_ref[...] = jnp.zeros_like(acc_ref)

    def tile(s0, r0, r1, width, mask):
        zs = []
        for c in range(2):
            z = _dot_nt(q_ref[0, r0:r1, c * HEAD_DIM:(c + 1) * HEAD_DIM],
                        k_ref[0, pl.ds(s0, width), c * HEAD_DIM:(c + 1) * HEAD_DIM])
            zs.append(z if mask is None else jnp.where(mask, z, NEG_BIG))
        v = v_ref[0, pl.ds(s0, width), :]
        for c in range(2):
            m_old = m_ref[c, r0:r1, :]
            m_new = jnp.maximum(m_old, jnp.max(zs[c], axis=-1, keepdims=True))
            alpha = jnp.exp2(m_old - m_new)
            p = jnp.exp2(zs[c] - m_new)
            l_ref[c, r0:r1, :] = alpha * l_ref[c, r0:r1, :] + jnp.sum(p, axis=-1, keepdims=True)
            acc_ref[c, r0:r1, :] = alpha * acc_ref[c, r0:r1, :] + _dot(p.astype(BF16), v)
            m_ref[c, r0:r1, :] = m_new

    half = tq // 2
    row = lax.broadcasted_iota(jnp.int32, (half, half), 0)
    col = lax.broadcasted_iota(jnp.int32, (half, half), 1)
    lower = col <= row
    diag = pl.multiple_of(i * tq, tq)
    tile(diag, 0, half, half, lower)
    tile(diag, half, tq, tq, jnp.concatenate([jnp.ones((half, half), jnp.bool_), lower], axis=1))

    def body(step, _):
        tile(pl.multiple_of(step * tq, tq), 0, tq, tq, None)
        return 0

    lax.fori_loop(0, i, body, 0)

    lam = (jnp.exp(jnp.sum(lq1_ref[...] * lk1_ref[...])) - jnp.exp(jnp.sum(lq2_ref[...] * lk2_ref[...]))
           + lambda_init)
    o = acc_ref[0] / l_ref[0] - lam * (acc_ref[1] / l_ref[1])
    o = _rmsnorm_rows(o, sub_ref[...]) * (1.0 - lambda_init)
    o_ref[0] = o.astype(o_ref.dtype)


def _diff_attention(proj, lq1, lk1, lq2, lk2, subln, lambda_init, *, tq):
    b, s, _ = proj.shape
    assert s % tq == 0
    dv = 2 * HEAD_DIM
    qoff = 3 * SB_WIDTH // dv
    koff = qoff + N_DIFF_HEADS
    voff = koff + N_DIFF_HEADS
    vec = pl.BlockSpec((1, HEAD_DIM), lambda bi, h, i: (0, 0))
    return pl.pallas_call(
        functools.partial(_diff_kernel, tq=tq, lambda_init=lambda_init),
        out_shape=jax.ShapeDtypeStruct((b, s, DIFF_WIDTH), BF16),
        grid=(b, N_DIFF_HEADS, s // tq),
        in_specs=[
            pl.BlockSpec((1, tq, dv), lambda bi, h, i: (bi, i, qoff + h)),
            pl.BlockSpec((1, s, dv), lambda bi, h, i: (bi, 0, koff + h)),
            pl.BlockSpec((1, s, dv), lambda bi, h, i: (bi, 0, voff + h)),
            vec, vec, vec, vec,
            pl.BlockSpec((1, dv), lambda bi, h, i: (0, 0)),
        ],
        out_specs=pl.BlockSpec((1, tq, dv), lambda bi, h, i: (bi, i, h)),
        scratch_shapes=[
            pltpu.VMEM((2, tq, 1), F32), pltpu.VMEM((2, tq, 1), F32), pltpu.VMEM((2, tq, dv), F32),
        ],
        compiler_params=pltpu.CompilerParams(
            dimension_semantics=("parallel", "parallel", "arbitrary"),
            vmem_limit_bytes=56 * MIB,
        ),
        name="diff_attn",
    )(proj, proj, proj, lq1, lk1, lq2, lk2, subln)


def _outproj_kernel(x_ref, sb_ref, df_ref, w1_ref, w2_ref, o_ref):
    o_ref[...] = x_ref[...] + _dot(sb_ref[...], w1_ref[...]) + _dot(df_ref[...], w2_ref[...])


def _outproj(x, sb, df, w_out, *, tm=512, tn=2048):
    t, d = x.shape
    assert t % tm == 0 and d % tn == 0
    k1 = sb.shape[1]
    k2 = df.shape[1]
    assert k1 % tn == 0 or tn % k1 == 0
    return pl.pallas_call(
        _outproj_kernel,
        out_shape=jax.ShapeDtypeStruct((t, d), F32),
        grid=(t // tm, d // tn),
        in_specs=[
            pl.BlockSpec((tm, tn), lambda i, j: (i, j)),
            pl.BlockSpec((tm, k1), lambda i, j: (i, 0)),
            pl.BlockSpec((tm, k2), lambda i, j: (i, 0)),
            pl.BlockSpec((k1, tn), lambda i, j: (0, j)),
            pl.BlockSpec((k2, tn), lambda i, j: (1, j)),
        ],
        out_specs=pl.BlockSpec((tm, tn), lambda i, j: (i, j)),
        compiler_params=pltpu.CompilerParams(
            dimension_semantics=("parallel", "arbitrary"),
            vmem_limit_bytes=48 * MIB,
        ),
        name="outproj_residual",
    )(x, sb, df, w_out, w_out)


def _rope_tables(seq):
    pos = jnp.arange(seq, dtype=F32)
    inv_freq = ROPE_THETA ** (-jnp.arange(0, HEAD_DIM, 2, dtype=F32) / HEAD_DIM)
    ang = pos[:, None] * inv_freq[None, :]
    ang = jnp.concatenate([ang, ang], axis=-1)
    sign = jnp.concatenate([-jnp.ones((HEAD_DIM // 2,), F32), jnp.ones((HEAD_DIM // 2,), F32)])
    return jnp.cos(ang), jnp.sin(ang) * sign[None, :]


def kernel(x, norm_ffn1, w_ffn1_gate, w_ffn1_up, w_ffn1_down, norm_mix, w_in, lambda_q1, lambda_k1, lambda_q2, lambda_k2, diff_subln, w_out, norm_ffn2, w_ffn2_gate, w_ffn2_up, w_ffn2_down, norm_final):
    b, s, d = x.shape
    depth = w_in.shape[0]
    cos, sin = _rope_tables(s)
    tq, tk = 1024, 256
    tri = (lax.broadcasted_iota(jnp.int32, (tk, tk), 0) > lax.broadcasted_iota(jnp.int32, (tk, tk), 1)).astype(BF16)
    g_final = norm_final.reshape(1, d)

    xt = x.reshape(b * s, d)
    for layer in range(depth):
        lambda_init = 0.8 - 0.6 * math.exp(-0.3 * layer)
        xt, (w_in16, w_out16) = _ffn(xt, norm_ffn1[layer].reshape(1, d), w_ffn1_gate, w_ffn1_up, w_ffn1_down,
                                     g_final, layer, final_norm=False, side=(w_in, w_out))
        proj = _inproj(xt, norm_mix[layer].reshape(1, d), w_in16, cos, sin, s)
        proj = proj.reshape(b, s, -1)
        sb = _sb_attention(proj, tri, tq=tq, tk=tk)
        df = _diff_attention(proj, lambda_q1[layer].reshape(1, -1), lambda_k1[layer].reshape(1, -1),
                             lambda_q2[layer].reshape(1, -1), lambda_k2[layer].reshape(1, -1),
                             diff_subln[layer].reshape(1, -1), lambda_init, tq=tq)
        xt = _outproj(xt, sb.reshape(b * s, -1), df.reshape(b * s, -1), w_out16)
        xt, _ = _ffn(xt, norm_ffn2[layer].reshape(1, d), w_ffn2_gate, w_ffn2_up, w_ffn2_down, g_final, layer,
                     final_norm=(layer == depth - 1))
    return xt.reshape(b, s, d)
```

```python
import functools
import math

import jax
import jax.numpy as jnp
from jax import lax
from jax.experimental import pallas as pl
from jax.experimental.pallas import tpu as pltpu

HEAD_DIM = 128
N_SB_HEADS = 8
N_DIFF_HEADS = 4
SB_WIDTH = N_SB_HEADS * HEAD_DIM
DIFF_WIDTH = N_DIFF_HEADS * 2 * HEAD_DIM
ROPE_THETA = 10000.0
RMS_EPS = 1e-6
FFN_RESIDUAL_WEIGHT = 0.5
LOG2E = math.log2(math.e)
Q_SCALE = LOG2E / math.sqrt(HEAD_DIM)

SIDE_CAST_STEPS = 64
SOFTPLUS2_LINEAR = 64.0
DIFF_ROW_CHUNKS = 4
V7X_VMEM_BYTES = 64 * 1024 * 1024
MIB = 1024 * 1024
NEG_BIG = -1e30

F32 = jnp.float32
BF16 = jnp.bfloat16


def _rmsnorm_rows(x, g):
    ms = jnp.mean(x * x, axis=-1, keepdims=True)
    return x * lax.rsqrt(ms + RMS_EPS) * g


def _dot(a, b):
    return jnp.dot(a, b, preferred_element_type=F32)


def _dot_nt(a, b):
    return lax.dot_general(a, b, (((1,), (1,)), ((), ())), preferred_element_type=F32)


def _ffn_kernel(x_ref, g_ref, wg_ref, wu_ref, wd_ref, gf_ref, *rest, final_norm, emit_bf16, n_side, side_steps):
    side_in = rest[:n_side]
    o_ref = rest[n_side]
    w16_refs = rest[n_side + 1:n_side + 4] if emit_bf16 else ()
    side_out = rest[len(rest) - 1 - n_side:len(rest) - 1]
    h_ref = rest[-1]
    i = pl.program_id(0)
    j = pl.program_id(1)
    nj = pl.num_programs(1)

    @pl.when(j == 0)
    def _():
        x = x_ref[...]
        h_ref[...] = _rmsnorm_rows(x, g_ref[...]).astype(BF16)
        o_ref[...] = x

    if n_side:
        @pl.when(i * nj + j < side_steps)
        def _():
            for src, dst in zip(side_in, side_out):
                dst[...] = src[...].astype(BF16)

    wg = wg_ref[...].astype(BF16)
    wu = wu_ref[...].astype(BF16)
    wd = wd_ref[...].astype(BF16)
    if emit_bf16:
        wg16_ref, wu16_ref, wd16_ref = w16_refs
        wg16_ref[...] = wg
        wu16_ref[...] = wu
        wd16_ref[...] = wd

    h = h_ref[...]
    gate = _dot(h, wg)
    up = _dot(h, wu)
    act = (gate / (1.0 + jnp.exp(-gate))) * (up * FFN_RESIDUAL_WEIGHT)
    o_ref[...] += _dot(act.astype(BF16), wd)

    if final_norm:
        @pl.when(j == nj - 1)
        def _():
            o_ref[...] = _rmsnorm_rows(o_ref[...], gf_ref[...])


def _ffn_call(x, g, wg, wu, wd, g_final, *, layer, final_norm, row_block0, n_row_blocks, tm, tf,
              side=(), side_layer=None):
    t, d = x.shape
    emit_bf16 = layer is not None
    d_ff = wg.shape[-1]
    nj = d_ff // tf
    assert d_ff % tf == 0 and (row_block0 + n_row_blocks) * tm <= t
    if emit_bf16:
        w_in_specs = [
            pl.BlockSpec((None, d, tf), lambda i, j: (layer, 0, j)),
            pl.BlockSpec((None, d, tf), lambda i, j: (layer, 0, j)),
            pl.BlockSpec((None, tf, d), lambda i, j: (layer, j, 0)),
        ]
    else:
        w_in_specs = [
            pl.BlockSpec((d, tf), lambda i, j: (0, j)),
            pl.BlockSpec((d, tf), lambda i, j: (0, j)),
            pl.BlockSpec((tf, d), lambda i, j: (j, 0)),
        ]
    out_shape = [jax.ShapeDtypeStruct((t, d), F32)]
    out_specs = [pl.BlockSpec((tm, d), lambda i, j: (i + row_block0, 0))]
    if emit_bf16:
        out_shape += [jax.ShapeDtypeStruct((d, d_ff), BF16), jax.ShapeDtypeStruct((d, d_ff), BF16),
                      jax.ShapeDtypeStruct((d_ff, d), BF16)]
        out_specs += [pl.BlockSpec((d, tf), lambda i, j: (0, j)), pl.BlockSpec((d, tf), lambda i, j: (0, j)),
                      pl.BlockSpec((tf, d), lambda i, j: (j, 0))]
    side_steps = SIDE_CAST_STEPS if side else 0
    assert side_steps <= n_row_blocks * nj
    side_in_specs = []
    for arr in side:
        _, rows, cols = arr.shape
        assert rows % side_steps == 0
        slab = rows // side_steps
        side_in_specs.append(pl.BlockSpec(
            (None, slab, cols), lambda i, j: (side_layer, jnp.minimum(i * nj + j, side_steps - 1), 0)))
        out_shape.append(jax.ShapeDtypeStruct((rows, cols), BF16))
        out_specs.append(pl.BlockSpec((slab, cols), lambda i, j: (jnp.minimum(i * nj + j, side_steps - 1), 0)))
    x_mode = dict(pipeline_mode=pl.Buffered(1)) if n_row_blocks == 1 else {}
    res = pl.pallas_call(
        functools.partial(_ffn_kernel, final_norm=final_norm, emit_bf16=emit_bf16, n_side=len(side),
                          side_steps=side_steps),
        out_shape=out_shape,
        grid=(n_row_blocks, nj),
        in_specs=[
            pl.BlockSpec((tm, d), lambda i, j: (i + row_block0, 0), **x_mode),
            pl.BlockSpec((1, d), lambda i, j: (0, 0)),
            *w_in_specs,
            pl.BlockSpec((1, d), lambda i, j: (0, 0)),
            *side_in_specs,
        ],
        out_specs=out_specs,
        scratch_shapes=[pltpu.VMEM((tm, d), BF16)],
        input_output_aliases={0: 0},
        compiler_params=pltpu.CompilerParams(
            dimension_semantics=("arbitrary", "arbitrary"),
            vmem_limit_bytes=58 * MIB,
        ),
        name="ffn_swiglu_cast" if emit_bf16 else "ffn_swiglu",
    )(x, g, wg, wu, wd, g_final, *side)
    return res


def _ffn(x, g, wg32, wu32, wd32, g_final, layer, *, final_norm, side=(), tm=1024, tf=512, tf_first=256):
    t, _ = x.shape
    x, wg, wu, wd = _ffn_call(x, g, wg32, wu32, wd32, g_final, layer=layer, final_norm=final_norm,
                              row_block0=0, n_row_blocks=1, tm=tm, tf=tf_first)
    x, *side16 = _ffn_call(x, g, wg, wu, wd, g_final, layer=None, final_norm=final_norm,
                           row_block0=1, n_row_blocks=t // tm - 1, tm=tm, tf=tf, side=side, side_layer=layer)
    return x, side16


def _inproj_kernel(x_ref, g_ref, w_ref, cos_ref, sin_ref, o_ref, h_ref, *, tn):
    j = pl.program_id(1)

    @pl.when(j == 0)
    def _():
        h_ref[...] = _rmsnorm_rows(x_ref[...], g_ref[...]).astype(BF16)

    acc = _dot(h_ref[...], w_ref[...])
    col = j * tn
    sbq_end = SB_WIDTH
    dq_start = 3 * SB_WIDTH
    dk_start = dq_start + DIFF_WIDTH
    dv_start = dk_start + DIFF_WIDTH

    def rope(y):
        cos = cos_ref[...]
        sin = sin_ref[...]
        parts = []
        for c in range(tn // HEAD_DIM):
            yc = y[:, c * HEAD_DIM:(c + 1) * HEAD_DIM]
            parts.append(yc * cos + pltpu.roll(yc, HEAD_DIM // 2, 1) * sin)
        return jnp.concatenate(parts, axis=1)

    @pl.when(col < sbq_end)
    def _():
        o_ref[...] = (acc * Q_SCALE).astype(o_ref.dtype)

    @pl.when(jnp.logical_or(jnp.logical_and(col >= sbq_end, col < dq_start), col >= dv_start))
    def _():
        o_ref[...] = acc.astype(o_ref.dtype)

    @pl.when(jnp.logical_and(col >= dq_start, col < dk_start))
    def _():
        o_ref[...] = (rope(acc) * Q_SCALE).astype(o_ref.dtype)

    @pl.when(jnp.logical_and(col >= dk_start, col < dv_start))
    def _():
        o_ref[...] = rope(acc).astype(o_ref.dtype)


def _inproj(x, g, w, cos, sin, seq, *, tm=1024, tn=1024):
    t, d = x.shape
    d_in = w.shape[1]
    assert t % tm == 0 and d_in % tn == 0 and seq % tm == 0 and SB_WIDTH % tn == 0
    pos_blocks = seq // tm
    return pl.pallas_call(
        functools.partial(_inproj_kernel, tn=tn),
        out_shape=jax.ShapeDtypeStruct((t, d_in), BF16),
        grid=(t // tm, d_in // tn),
        in_specs=[
            pl.BlockSpec((tm, d), lambda i, j: (i, 0)),
            pl.BlockSpec((1, d), lambda i, j: (0, 0)),
            pl.BlockSpec((d, tn), lambda i, j: (0, j)),
            pl.BlockSpec((tm, HEAD_DIM), lambda i, j: (i % pos_blocks, 0)),
            pl.BlockSpec((tm, HEAD_DIM), lambda i, j: (i % pos_blocks, 0)),
        ],
        out_specs=pl.BlockSpec((tm, tn), lambda i, j: (i, j)),
        scratch_shapes=[pltpu.VMEM((tm, d), BF16)],
        compiler_params=pltpu.CompilerParams(
            dimension_semantics=("parallel", "arbitrary"),
            vmem_limit_bytes=48 * MIB,
        ),
        name="inproj_rope",
    )(x, g, w, cos, sin)


def _mask_top_rows(x, mask, fill):
    n = mask.shape[0]
    top = jnp.where(mask, x[:n], fill)
    return top if n == x.shape[0] else jnp.concatenate([top, x[n:]], axis=0)


def _sb_tile(q, k, v, tri, carry, mask):
    z = _dot_nt(q, k)
    sp = jnp.where(z > SOFTPLUS2_LINEAR, z, jnp.log(1.0 + jnp.exp2(z)) * LOG2E)
    if mask is not None:
        sp = _mask_top_rows(sp, mask, 0.0)
    excl = _dot(sp.astype(BF16), tri)
    w = jnp.exp2((z - sp) - excl - carry)
    if mask is not None:
        w = _mask_top_rows(w, mask, 0.0)
    out = _dot(w.astype(BF16), v)
    return out, carry + (excl[:, 0:1] + sp[:, 0:1])


def _sb_phases(q, k, v, tri, acc_ref, carry_ref, i, *, tq, tk):
    nb = tq // tk

    def band():
        row = lax.broadcasted_iota(jnp.int32, (tk, tk), 0)
        col = lax.broadcasted_iota(jnp.int32, (tk, tk), 1)
        strict_lower = col < row
        acc = None
        carry = None
        for d in reversed(range(nb)):
            r0 = d * tk
            s0 = pl.multiple_of(i * tq + r0, tk)
            zero_c = jnp.zeros((tk, 1), F32)
            carry_in = zero_c if carry is None else jnp.concatenate([zero_c, carry], axis=0)
            out, carry = _sb_tile(q[r0:, :], k[pl.ds(s0, tk), :], v[pl.ds(s0, tk), :], tri, carry_in, strict_lower)
            acc = out if acc is None else out + jnp.concatenate([jnp.zeros((tk, HEAD_DIM), F32), acc], axis=0)
        acc_ref[...] = acc
        carry_ref[...] = carry

    def trip(step):
        qv = q[...]
        carry = carry_ref[...]
        total = None
        for u in range(nb):
            s0 = pl.multiple_of(i * tq - (step * nb + u + 1) * tk, tk)
            out, carry = _sb_tile(qv, k[pl.ds(s0, tk), :], v[pl.ds(s0, tk), :], tri, carry, None)
            total = out if total is None else total + out
            if u < nb - 1:
                yield
        acc_ref[...] += total
        carry_ref[...] = carry

    return band, trip


def _diff_phases(q, k, v, m_ref, l_ref, acc_ref, i, *, tq):
    def tile(s0, r0, r1, width, mask):
        zs = []
        for c in range(2):
            z = _dot_nt(q[r0:r1, c * HEAD_DIM:(c + 1) * HEAD_DIM], k[pl.ds(s0, width), c * HEAD_DIM:(c + 1) * HEAD_DIM])
            zs.append(z if mask is None else jnp.where(mask, z, NEG_BIG))
        vv = v[pl.ds(s0, width), :]
        for c in range(2):
            m_old = m_ref[c, r0:r1, :]
            m_new = jnp.maximum(m_old, jnp.max(zs[c], axis=-1, keepdims=True))
            alpha = jnp.exp2(m_old - m_new)
            p = jnp.exp2(zs[c] - m_new)
            l_ref[c, r0:r1, :] = alpha * l_ref[c, r0:r1, :] + jnp.sum(p, axis=-1, keepdims=True)
            acc_ref[c, r0:r1, :] = alpha * acc_ref[c, r0:r1, :] + _dot(p.astype(BF16), vv)
            m_ref[c, r0:r1, :] = m_new

    def band():
        m_ref[...] = jnp.full_like(m_ref, NEG_BIG)
        l_ref[...] = jnp.zeros_like(l_ref)
        acc_ref[...] = jnp.zeros_like(acc_ref)
        half = tq // 2
        row = lax.broadcasted_iota(jnp.int32, (half, half), 0)
        col = lax.broadcasted_iota(jnp.int32, (half, half), 1)
        lower = col <= row
        diag = pl.multiple_of(i * tq, tq)
        tile(diag, 0, half, half, lower)
        tile(diag, half, tq, tq, jnp.concatenate([jnp.ones((half, half), jnp.bool_), lower], axis=1))

    def trip(step):
        s0 = pl.multiple_of(i * tq - (step + 1) * tq, tq)
        rows = tq // DIFF_ROW_CHUNKS
        for c in range(DIFF_ROW_CHUNKS):
            tile(s0, c * rows, (c + 1) * rows, tq, None)
            if c < DIFF_ROW_CHUNKS - 1:
                yield

    return band, trip


def _diff_finish(acc_ref, l_ref, lq1_ref, lk1_ref, lq2_ref, lk2_ref, sub_ref, lambda_init):
    lam = (jnp.exp(jnp.sum(lq1_ref[...] * lk1_ref[...])) - jnp.exp(jnp.sum(lq2_ref[...] * lk2_ref[...]))
           + lambda_init)
    o = acc_ref[0] / l_ref[0] - lam * (acc_ref[1] / l_ref[1])
    return _rmsnorm_rows(o, sub_ref[...]) * (1.0 - lambda_init)


def _attn_kernel(sq_ref, sk_ref, sv_ref, tri_ref, dq_ref, dk_ref, dv_ref, lq1_ref, lk1_ref, lq2_ref, lk2_ref,
                 sub_ref, so_ref, do_ref, sacc_ref, scarry_ref, m_ref, l_ref, dacc_ref, *, tq, tk, sb_heads,
                 lambda_init):
    i = pl.program_id(2)
    tri = tri_ref[...]
    phases = []
    for e in range(sb_heads):
        cols = slice(e * HEAD_DIM, (e + 1) * HEAD_DIM)
        phases.append(_sb_phases(sq_ref.at[0, :, cols], sk_ref.at[0, :, cols], sv_ref.at[0, :, cols], tri,
                                 sacc_ref.at[e], scarry_ref.at[e], i, tq=tq, tk=tk))
    phases.append(_diff_phases(dq_ref.at[0], dk_ref.at[0], dv_ref.at[0], m_ref, l_ref, dacc_ref, i, tq=tq))

    for band, _ in phases:
        band()

    def body(step, _):
        live = [trip(step) for _, trip in phases]
        while live:
            live = [g for g in live if next(g, StopIteration) is not StopIteration]
        return 0

    lax.fori_loop(0, i, body, 0)

    for e in range(sb_heads):
        so_ref[0, :, e * HEAD_DIM:(e + 1) * HEAD_DIM] = sacc_ref[e].astype(so_ref.dtype)
    o = _diff_finish(dacc_ref, l_ref, lq1_ref, lk1_ref, lq2_ref, lk2_ref, sub_ref, lambda_init)
    do_ref[0] = o.astype(do_ref.dtype)


def _attention(proj, tri, lq1, lk1, lq2, lk2, subln, lambda_init, *, tq, tk):
    b, s, _ = proj.shape
    assert s % tq == 0 and tq % tk == 0 and tri.shape == (tk, tk) and N_SB_HEADS % N_DIFF_HEADS == 0
    sb_heads = N_SB_HEADS // N_DIFF_HEADS
    sw = sb_heads * HEAD_DIM
    dv = 2 * HEAD_DIM
    s_q, s_k, s_v = 0, SB_WIDTH // sw, 2 * SB_WIDTH // sw
    d_q = 3 * SB_WIDTH // dv
    d_k = d_q + N_DIFF_HEADS
    d_v = d_k + N_DIFF_HEADS
    vec = pl.BlockSpec((1, HEAD_DIM), lambda bi, h, i: (0, 0))
    return pl.pallas_call(
        functools.partial(_attn_kernel, tq=tq, tk=tk, sb_heads=sb_heads, lambda_init=lambda_init),
        out_shape=[jax.ShapeDtypeStruct((b, s, SB_WIDTH), BF16), jax.ShapeDtypeStruct((b, s, DIFF_WIDTH), BF16)],
        grid=(b, N_DIFF_HEADS, s // tq),
        in_specs=[
            pl.BlockSpec((1, tq, sw), lambda bi, h, i: (bi, i, s_q + h)),
            pl.BlockSpec((1, s, sw), lambda bi, h, i: (bi, 0, s_k + h)),
            pl.BlockSpec((1, s, sw), lambda bi, h, i: (bi, 0, s_v + h)),
            pl.BlockSpec((tk, tk), lambda bi, h, i: (0, 0)),
            pl.BlockSpec((1, tq, dv), lambda bi, h, i: (bi, i, d_q + h)),
            pl.BlockSpec((1, s, dv), lambda bi, h, i: (bi, 0, d_k + h)),
            pl.BlockSpec((1, s, dv), lambda bi, h, i: (bi, 0, d_v + h)),
            vec, vec, vec, vec,
            pl.BlockSpec((1, dv), lambda bi, h, i: (0, 0)),
        ],
        out_specs=[pl.BlockSpec((1, tq, sw), lambda bi, h, i: (bi, i, h)),
                   pl.BlockSpec((1, tq, dv), lambda bi, h, i: (bi, i, h))],
        scratch_shapes=[
            pltpu.VMEM((sb_heads, tq, HEAD_DIM), F32), pltpu.VMEM((sb_heads, tq, 1), F32),
            pltpu.VMEM((2, tq, 1), F32), pltpu.VMEM((2, tq, 1), F32), pltpu.VMEM((2, tq, dv), F32),
        ],
        compiler_params=pltpu.CompilerParams(
            dimension_semantics=("parallel", "parallel", "arbitrary"),
            vmem_limit_bytes=56 * MIB,
        ),
        name="attn_sb_diff",
    )(proj, proj, proj, tri, proj, proj, proj, lq1, lk1, lq2, lk2, subln)


def _outproj_kernel(x_ref, sb_ref, df_ref, w1_ref, w2_ref, o_ref):
    o_ref[...] = x_ref[...] + _dot(sb_ref[...], w1_ref[...]) + _dot(df_ref[...], w2_ref[...])


def _outproj(x, sb, df, w_out, *, tm=512, tn=2048):
    t, d = x.shape
    assert t % tm == 0 and d % tn == 0
    k1 = sb.shape[1]
    k2 = df.shape[1]
    assert k1 % tn == 0 or tn % k1 == 0
    return pl.pallas_call(
        _outproj_kernel,
        out_shape=jax.ShapeDtypeStruct((t, d), F32),
        grid=(t // tm, d // tn),
        in_specs=[
            pl.BlockSpec((tm, tn), lambda i, j: (i, j)),
            pl.BlockSpec((tm, k1), lambda i, j: (i, 0)),
            pl.BlockSpec((tm, k2), lambda i, j: (i, 0)),
            pl.BlockSpec((k1, tn), lambda i, j: (0, j)),
            pl.BlockSpec((k2, tn), lambda i, j: (1, j)),
        ],
        out_specs=pl.BlockSpec((tm, tn), lambda i, j: (i, j)),
        compiler_params=pltpu.CompilerParams(
            dimension_semantics=("parallel", "arbitrary"),
            vmem_limit_bytes=48 * MIB,
        ),
        name="outproj_residual",
    )(x, sb, df, w_out, w_out)


def _rope_tables(seq):
    pos = jnp.arange(seq, dtype=F32)
    inv_freq = ROPE_THETA ** (-jnp.arange(0, HEAD_DIM, 2, dtype=F32) / HEAD_DIM)
    ang = pos[:, None] * inv_freq[None, :]
    ang = jnp.concatenate([ang, ang], axis=-1)
    sign = jnp.concatenate([-jnp.ones((HEAD_DIM // 2,), F32), jnp.ones((HEAD_DIM // 2,), F32)])
    return jnp.cos(ang), jnp.sin(ang) * sign[None, :]


def kernel(x, norm_ffn1, w_ffn1_gate, w_ffn1_up, w_ffn1_down, norm_mix, w_in, lambda_q1, lambda_k1, lambda_q2, lambda_k2, diff_subln, w_out, norm_ffn2, w_ffn2_gate, w_ffn2_up, w_ffn2_down, norm_final):
    b, s, d = x.shape
    depth = w_in.shape[0]
    cos, sin = _rope_tables(s)
    tq, tk = 1024, 256
    tri = (lax.broadcasted_iota(jnp.int32, (tk, tk), 0) > lax.broadcasted_iota(jnp.int32, (tk, tk), 1)).astype(BF16)
    g_final = norm_final.reshape(1, d)

    xt = x.reshape(b * s, d)
    for layer in range(depth):
        lambda_init = 0.8 - 0.6 * math.exp(-0.3 * layer)
        xt, (w_in16, w_out16) = _ffn(xt, norm_ffn1[layer].reshape(1, d), w_ffn1_gate, w_ffn1_up, w_ffn1_down,
                                     g_final, layer, final_norm=False, side=(w_in, w_out))
        proj = _inproj(xt, norm_mix[layer].reshape(1, d), w_in16, cos, sin, s)
        proj = proj.reshape(b, s, -1)
        sb, df = _attention(proj, tri, lambda_q1[layer].reshape(1, -1), lambda_k1[layer].reshape(1, -1),
                            lambda_q2[layer].reshape(1, -1), lambda_k2[layer].reshape(1, -1),
                            diff_subln[layer].reshape(1, -1), lambda_init, tq=tq, tk=tk)
        xt = _outproj(xt, sb.reshape(b * s, -1), df.reshape(b * s, -1), w_out16)
        xt, _ = _ffn(xt, norm_ffn2[layer].reshape(1, d), w_ffn2_gate, w_ffn2_up, w_ffn2_down, g_final, layer,
                     final_norm=(layer == depth - 1))
    return xt.reshape(b, s, d)
```

```python
import functools
import math

import jax
import jax.numpy as jnp
from jax import lax
from jax.experimental import pallas as pl
from jax.experimental.pallas import tpu as pltpu

HEAD_DIM = 128
N_SB_HEADS = 8
N_DIFF_HEADS = 4
SB_WIDTH = N_SB_HEADS * HEAD_DIM
DIFF_WIDTH = N_DIFF_HEADS * 2 * HEAD_DIM
ROPE_THETA = 10000.0
RMS_EPS = 1e-6
FFN_RESIDUAL_WEIGHT = 0.5
LOG2E = math.log2(math.e)
Q_SCALE = LOG2E / math.sqrt(HEAD_DIM)

SIDE_CAST_STEPS = 64
SOFTPLUS2_LINEAR = 64.0
INPROJ_ROW_CHUNKS = 4
DIFF_ROW_CHUNKS = 4
V7X_VMEM_BYTES = 64 * 1024 * 1024
MIB = 1024 * 1024
NEG_BIG = -1e30

F32 = jnp.float32
BF16 = jnp.bfloat16


def _vmem_limit(mib):
    limit = mib * MIB
    assert limit < V7X_VMEM_BYTES
    return limit


def _rmsnorm_rows(x, g):
    ms = jnp.mean(x * x, axis=-1, keepdims=True)
    return x * lax.rsqrt(ms + RMS_EPS) * g


def _dot(a, b):
    return jnp.dot(a, b, preferred_element_type=F32)


def _dot_nt(a, b):
    return lax.dot_general(a, b, (((1,), (1,)), ((), ())), preferred_element_type=F32)


def _ffn_kernel(x_ref, g_ref, wg_ref, wu_ref, wd_ref, gf_ref, *rest, final_norm, emit_bf16, n_side, side_steps):
    side_in = rest[:n_side]
    o_ref = rest[n_side]
    w16_refs = rest[n_side + 1:n_side + 4] if emit_bf16 else ()
    side_out = rest[len(rest) - 1 - n_side:len(rest) - 1]
    h_ref = rest[-1]
    i = pl.program_id(0)
    j = pl.program_id(1)
    nj = pl.num_programs(1)

    @pl.when(j == 0)
    def _():
        x = x_ref[...]
        h_ref[...] = _rmsnorm_rows(x, g_ref[...]).astype(BF16)
        o_ref[...] = x

    if n_side:
        @pl.when(i * nj + j < side_steps)
        def _():
            for src, dst in zip(side_in, side_out):
                dst[...] = src[...].astype(BF16)

    wg = wg_ref[...].astype(BF16)
    wu = wu_ref[...].astype(BF16)
    wd = wd_ref[...].astype(BF16)
    if emit_bf16:
        wg16_ref, wu16_ref, wd16_ref = w16_refs
        wg16_ref[...] = wg
        wu16_ref[...] = wu
        wd16_ref[...] = wd

    h = h_ref[...]
    gate = _dot(h, wg)
    up = _dot(h, wu)
    act = (gate / (1.0 + jnp.exp(-gate))) * (up * FFN_RESIDUAL_WEIGHT)
    o_ref[...] += _dot(act.astype(BF16), wd)

    if final_norm:
        @pl.when(j == nj - 1)
        def _():
            o_ref[...] = _rmsnorm_rows(o_ref[...], gf_ref[...])


def _ffn_call(x, g, wg, wu, wd, g_final, *, layer, final_norm, row_block0, n_row_blocks, tm, tf,
              side=(), side_layer=None):
    t, d = x.shape
    emit_bf16 = layer is not None
    d_ff = wg.shape[-1]
    nj = d_ff // tf
    assert d_ff % tf == 0 and (row_block0 + n_row_blocks) * tm <= t
    if emit_bf16:
        w_in_specs = [
            pl.BlockSpec((None, d, tf), lambda i, j: (layer, 0, j)),
            pl.BlockSpec((None, d, tf), lambda i, j: (layer, 0, j)),
            pl.BlockSpec((None, tf, d), lambda i, j: (layer, j, 0)),
        ]
    else:
        w_in_specs = [
            pl.BlockSpec((d, tf), lambda i, j: (0, j)),
            pl.BlockSpec((d, tf), lambda i, j: (0, j)),
            pl.BlockSpec((tf, d), lambda i, j: (j, 0)),
        ]
    out_shape = [jax.ShapeDtypeStruct((t, d), F32)]
    out_specs = [pl.BlockSpec((tm, d), lambda i, j: (i + row_block0, 0))]
    if emit_bf16:
        out_shape += [jax.ShapeDtypeStruct((d, d_ff), BF16), jax.ShapeDtypeStruct((d, d_ff), BF16),
                      jax.ShapeDtypeStruct((d_ff, d), BF16)]
        out_specs += [pl.BlockSpec((d, tf), lambda i, j: (0, j)), pl.BlockSpec((d, tf), lambda i, j: (0, j)),
                      pl.BlockSpec((tf, d), lambda i, j: (j, 0))]
    side_steps = SIDE_CAST_STEPS if side else 0
    assert side_steps <= n_row_blocks * nj
    side_in_specs = []
    for arr in side:
        _, rows, cols = arr.shape
        assert rows % side_steps == 0
        slab = rows // side_steps
        side_in_specs.append(pl.BlockSpec(
            (None, slab, cols), lambda i, j: (side_layer, jnp.minimum(i * nj + j, side_steps - 1), 0)))
        out_shape.append(jax.ShapeDtypeStruct((rows, cols), BF16))
        out_specs.append(pl.BlockSpec((slab, cols), lambda i, j: (jnp.minimum(i * nj + j, side_steps - 1), 0)))
    x_mode = dict(pipeline_mode=pl.Buffered(1)) if n_row_blocks == 1 else {}
    res = pl.pallas_call(
        functools.partial(_ffn_kernel, final_norm=final_norm, emit_bf16=emit_bf16, n_side=len(side),
                          side_steps=side_steps),
        out_shape=out_shape,
        grid=(n_row_blocks, nj),
        in_specs=[
            pl.BlockSpec((tm, d), lambda i, j: (i + row_block0, 0), **x_mode),
            pl.BlockSpec((1, d), lambda i, j: (0, 0)),
            *w_in_specs,
            pl.BlockSpec((1, d), lambda i, j: (0, 0)),
            *side_in_specs,
        ],
        out_specs=out_specs,
        scratch_shapes=[pltpu.VMEM((tm, d), BF16)],
        input_output_aliases={0: 0},
        compiler_params=pltpu.CompilerParams(
            dimension_semantics=("arbitrary", "arbitrary"),
            vmem_limit_bytes=_vmem_limit(58),
        ),
        name="ffn_swiglu_cast" if emit_bf16 else "ffn_swiglu",
    )(x, g, wg, wu, wd, g_final, *side)
    return res


def _ffn(x, g, wg32, wu32, wd32, g_final, layer, *, final_norm, side=(), tm=1024, tf=512, tf_first=256):
    t, _ = x.shape
    x, wg, wu, wd = _ffn_call(x, g, wg32, wu32, wd32, g_final, layer=layer, final_norm=final_norm,
                              row_block0=0, n_row_blocks=1, tm=tm, tf=tf_first)
    x, *side16 = _ffn_call(x, g, wg, wu, wd, g_final, layer=None, final_norm=final_norm,
                           row_block0=1, n_row_blocks=t // tm - 1, tm=tm, tf=tf, side=side, side_layer=layer)
    return x, side16


def _inproj_kernel(x_ref, g_ref, w_ref, a_ref, b_ref, o_ref, h_ref, *, tm, tn):
    j = pl.program_id(1)

    @pl.when(j == 0)
    def _():
        h_ref[...] = _rmsnorm_rows(x_ref[...], g_ref[...]).astype(BF16)

    rows = tm // INPROJ_ROW_CHUNKS
    for r in range(INPROJ_ROW_CHUNKS):
        rs = slice(r * rows, (r + 1) * rows)
        acc = _dot(h_ref[rs, :], w_ref[...])
        a = a_ref[rs, :]
        b = b_ref[rs, :]
        parts = []
        for c in range(tn // HEAD_DIM):
            yc = acc[:, c * HEAD_DIM:(c + 1) * HEAD_DIM]
            parts.append(yc * a + pltpu.roll(yc, HEAD_DIM // 2, 1) * b)
        o_ref[rs, :] = jnp.concatenate(parts, axis=1).astype(o_ref.dtype)


def _inproj_tables(cos, sin):
    one = jnp.ones_like(cos)
    zero = jnp.zeros_like(cos)
    return (jnp.stack([one, one * Q_SCALE, cos, cos * Q_SCALE]),
            jnp.stack([zero, zero, sin, sin * Q_SCALE]))


def _inproj(x, g, w, coef_a, coef_b, seq, *, tm=1024, tn=1024):
    t, d = x.shape
    d_in = w.shape[1]
    assert t % tm == 0 and d_in % tn == 0 and seq % tm == 0 and SB_WIDTH % tn == 0 and tm % INPROJ_ROW_CHUNKS == 0
    pos_blocks = seq // tm
    dq_start = 3 * SB_WIDTH
    dk_start = dq_start + DIFF_WIDTH
    dv_start = dk_start + DIFF_WIDTH

    def coef_index(i, j):
        col = j * tn
        mode = jnp.where(col < SB_WIDTH, 1,
                         jnp.where(col < dq_start, 0, jnp.where(col < dk_start, 3, jnp.where(col < dv_start, 2, 0))))
        return (mode, i % pos_blocks, 0)

    return pl.pallas_call(
        functools.partial(_inproj_kernel, tm=tm, tn=tn),
        out_shape=jax.ShapeDtypeStruct((t, d_in), BF16),
        grid=(t // tm, d_in // tn),
        in_specs=[
            pl.BlockSpec((tm, d), lambda i, j: (i, 0)),
            pl.BlockSpec((1, d), lambda i, j: (0, 0)),
            pl.BlockSpec((d, tn), lambda i, j: (0, j)),
            pl.BlockSpec((None, tm, HEAD_DIM), coef_index),
            pl.BlockSpec((None, tm, HEAD_DIM), coef_index),
        ],
        out_specs=pl.BlockSpec((tm, tn), lambda i, j: (i, j)),
        scratch_shapes=[pltpu.VMEM((tm, d), BF16)],
        compiler_params=pltpu.CompilerParams(
            dimension_semantics=("parallel", "arbitrary"),
            vmem_limit_bytes=_vmem_limit(48),
        ),
        name="inproj_rope",
    )(x, g, w, coef_a, coef_b)


def _mask_top_rows(x, mask, fill):
    n = mask.shape[0]
    top = jnp.where(mask, x[:n], fill)
    return top if n == x.shape[0] else jnp.concatenate([top, x[n:]], axis=0)


def _sb_tile(q, k, v, tri, carry, mask):
    z = _dot_nt(q, k)
    sp = jnp.where(z > SOFTPLUS2_LINEAR, z, jnp.log(1.0 + jnp.exp2(z)) * LOG2E)
    if mask is not None:
        sp = _mask_top_rows(sp, mask, 0.0)
    excl = _dot(sp.astype(BF16), tri)
    w = jnp.exp2((z - sp) - excl - carry)
    if mask is not None:
        w = _mask_top_rows(w, mask, 0.0)
    out = _dot(w.astype(BF16), v)
    return out, carry + (excl[:, 0:1] + sp[:, 0:1])


def _sb_phases(q, k, v, tri, acc_ref, carry_ref, i, *, tq, tk):
    nb = tq // tk

    def band():
        row = lax.broadcasted_iota(jnp.int32, (tk, tk), 0)
        col = lax.broadcasted_iota(jnp.int32, (tk, tk), 1)
        strict_lower = col < row
        acc = None
        carry = None
        for d in reversed(range(nb)):
            r0 = d * tk
            s0 = pl.multiple_of(i * tq + r0, tk)
            zero_c = jnp.zeros((tk, 1), F32)
            carry_in = zero_c if carry is None else jnp.concatenate([zero_c, carry], axis=0)
            out, carry = _sb_tile(q[r0:, :], k[pl.ds(s0, tk), :], v[pl.ds(s0, tk), :], tri, carry_in, strict_lower)
            acc = out if acc is None else out + jnp.concatenate([jnp.zeros((tk, HEAD_DIM), F32), acc], axis=0)
        acc_ref[...] = acc
        carry_ref[...] = carry

    def trip(step):
        qv = q[...]
        carry = carry_ref[...]
        total = None
        for u in range(nb):
            s0 = pl.multiple_of(i * tq - (step * nb + u + 1) * tk, tk)
            out, carry = _sb_tile(qv, k[pl.ds(s0, tk), :], v[pl.ds(s0, tk), :], tri, carry, None)
            total = out if total is None else total + out
            if u < nb - 1:
                yield
        acc_ref[...] += total
        carry_ref[...] = carry

    return band, trip


def _diff_phases(q, k, v, m_ref, l_ref, acc_ref, i, *, tq):
    def tile(s0, r0, r1, width, mask):
        zs = []
        for c in range(2):
            z = _dot_nt(q[r0:r1, c * HEAD_DIM:(c + 1) * HEAD_DIM], k[pl.ds(s0, width), c * HEAD_DIM:(c + 1) * HEAD_DIM])
            zs.append(z if mask is None else jnp.where(mask, z, NEG_BIG))
        vv = v[pl.ds(s0, width), :]
        for c in range(2):
            m_old = m_ref[c, r0:r1, :]
            m_new = jnp.maximum(m_old, jnp.max(zs[c], axis=-1, keepdims=True))
            alpha = jnp.exp2(m_old - m_new)
            p = jnp.exp2(zs[c] - m_new)
            l_ref[c, r0:r1, :] = alpha * l_ref[c, r0:r1, :] + jnp.sum(p, axis=-1, keepdims=True)
            acc_ref[c, r0:r1, :] = alpha * acc_ref[c, r0:r1, :] + _dot(p.astype(BF16), vv)
            m_ref[c, r0:r1, :] = m_new

    def band():
        m_ref[...] = jnp.full_like(m_ref, NEG_BIG)
        l_ref[...] = jnp.zeros_like(l_ref)
        acc_ref[...] = jnp.zeros_like(acc_ref)
        half = tq // 2
        row = lax.broadcasted_iota(jnp.int32, (half, half), 0)
        col = lax.broadcasted_iota(jnp.int32, (half, half), 1)
        lower = col <= row
        diag = pl.multiple_of(i * tq, tq)
        tile(diag, 0, half, half, lower)
        tile(diag, half, tq, tq, jnp.concatenate([jnp.ones((half, half), jnp.bool_), lower], axis=1))

    def trip(step):
        s0 = pl.multiple_of(i * tq - (step + 1) * tq, tq)
        rows = tq // DIFF_ROW_CHUNKS
        for c in range(DIFF_ROW_CHUNKS):
            tile(s0, c * rows, (c + 1) * rows, tq, None)
            if c < DIFF_ROW_CHUNKS - 1:
                yield

    return band, trip


def _diff_finish(acc_ref, l_ref, lq1_ref, lk1_ref, lq2_ref, lk2_ref, sub_ref, lambda_init):
    lam = (jnp.exp(jnp.sum(lq1_ref[...] * lk1_ref[...])) - jnp.exp(jnp.sum(lq2_ref[...] * lk2_ref[...]))
           + lambda_init)
    o = acc_ref[0] / l_ref[0] - lam * (acc_ref[1] / l_ref[1])
    return _rmsnorm_rows(o, sub_ref[...]) * (1.0 - lambda_init)


def _attn_kernel(sq_ref, sk_ref, sv_ref, tri_ref, dq_ref, dk_ref, dv_ref, lq1_ref, lk1_ref, lq2_ref, lk2_ref,
                 sub_ref, so_ref, do_ref, sacc_ref, scarry_ref, m_ref, l_ref, dacc_ref, *, tq, tk, sb_heads,
                 lambda_init):
    i = pl.program_id(2)
    tri = tri_ref[...]
    phases = []
    for e in range(sb_heads):
        cols = slice(e * HEAD_DIM, (e + 1) * HEAD_DIM)
        phases.append(_sb_phases(sq_ref.at[0, :, cols], sk_ref.at[0, :, cols], sv_ref.at[0, :, cols], tri,
                                 sacc_ref.at[e], scarry_ref.at[e], i, tq=tq, tk=tk))
    phases.append(_diff_phases(dq_ref.at[0], dk_ref.at[0], dv_ref.at[0], m_ref, l_ref, dacc_ref, i, tq=tq))

    for band, _ in phases:
        band()

    def body(step, _):
        live = [trip(step) for _, trip in phases]
        while live:
            live = [g for g in live if next(g, StopIteration) is not StopIteration]
        return 0

    lax.fori_loop(0, i, body, 0)

    for e in range(sb_heads):
        so_ref[0, :, e * HEAD_DIM:(e + 1) * HEAD_DIM] = sacc_ref[e].astype(so_ref.dtype)
    o = _diff_finish(dacc_ref, l_ref, lq1_ref, lk1_ref, lq2_ref, lk2_ref, sub_ref, lambda_init)
    do_ref[0] = o.astype(do_ref.dtype)


def _attention(proj, tri, lq1, lk1, lq2, lk2, subln, lambda_init, *, tq, tk):
    b, s, _ = proj.shape
    assert s % tq == 0 and tq % tk == 0 and tri.shape == (tk, tk) and N_SB_HEADS % N_DIFF_HEADS == 0
    sb_heads = N_SB_HEADS // N_DIFF_HEADS
    sw = sb_heads * HEAD_DIM
    dv = 2 * HEAD_DIM
    s_q, s_k, s_v = 0, SB_WIDTH // sw, 2 * SB_WIDTH // sw
    d_q = 3 * SB_WIDTH // dv
    d_k = d_q + N_DIFF_HEADS
    d_v = d_k + N_DIFF_HEADS
    vec = pl.BlockSpec((1, HEAD_DIM), lambda bi, h, i: (0, 0))
    return pl.pallas_call(
        functools.partial(_attn_kernel, tq=tq, tk=tk, sb_heads=sb_heads, lambda_init=lambda_init),
        out_shape=[jax.ShapeDtypeStruct((b, s, SB_WIDTH), BF16), jax.ShapeDtypeStruct((b, s, DIFF_WIDTH), BF16)],
        grid=(b, N_DIFF_HEADS, s // tq),
        in_specs=[
            pl.BlockSpec((1, tq, sw), lambda bi, h, i: (bi, i, s_q + h)),
            pl.BlockSpec((1, s, sw), lambda bi, h, i: (bi, 0, s_k + h)),
            pl.BlockSpec((1, s, sw), lambda bi, h, i: (bi, 0, s_v + h)),
            pl.BlockSpec((tk, tk), lambda bi, h, i: (0, 0)),
            pl.BlockSpec((1, tq, dv), lambda bi, h, i: (bi, i, d_q + h)),
            pl.BlockSpec((1, s, dv), lambda bi, h, i: (bi, 0, d_k + h)),
            pl.BlockSpec((1, s, dv), lambda bi, h, i: (bi, 0, d_v + h)),
            vec, vec, vec, vec,
            pl.BlockSpec((1, dv), lambda bi, h, i: (0, 0)),
        ],
        out_specs=[pl.BlockSpec((1, tq, sw), lambda bi, h, i: (bi, i, h)),
                   pl.BlockSpec((1, tq, dv), lambda bi, h, i: (bi, i, h))],
        scratch_shapes=[
            pltpu.VMEM((sb_heads, tq, HEAD_DIM), F32), pltpu.VMEM((sb_heads, tq, 1), F32),
            pltpu.VMEM((2, tq, 1), F32), pltpu.VMEM((2, tq, 1), F32), pltpu.VMEM((2, tq, dv), F32),
        ],
        compiler_params=pltpu.CompilerParams(
            dimension_semantics=("parallel", "parallel", "arbitrary"),
            vmem_limit_bytes=_vmem_limit(56),
        ),
        name="attn_sb_diff",
    )(proj, proj, proj, tri, proj, proj, proj, lq1, lk1, lq2, lk2, subln)


def _outproj_kernel(x_ref, sb_ref, df_ref, w1_ref, w2_ref, o_ref):
    o_ref[...] = x_ref[...] + _dot(sb_ref[...], w1_ref[...]) + _dot(df_ref[...], w2_ref[...])


def _outproj(x, sb, df, w_out, *, tm=512, tn=2048):
    t, d = x.shape
    assert t % tm == 0 and d % tn == 0
    k1 = sb.shape[1]
    k2 = df.shape[1]
    assert k1 % tn == 0 or tn % k1 == 0
    return pl.pallas_call(
        _outproj_kernel,
        out_shape=jax.ShapeDtypeStruct((t, d), F32),
        grid=(t // tm, d // tn),
        in_specs=[
            pl.BlockSpec((tm, tn), lambda i, j: (i, j)),
            pl.BlockSpec((tm, k1), lambda i, j: (i, 0)),
            pl.BlockSpec((tm, k2), lambda i, j: (i, 0)),
            pl.BlockSpec((k1, tn), lambda i, j: (0, j)),
            pl.BlockSpec((k2, tn), lambda i, j: (1, j)),
        ],
        out_specs=pl.BlockSpec((tm, tn), lambda i, j: (i, j)),
        compiler_params=pltpu.CompilerParams(
            dimension_semantics=("parallel", "arbitrary"),
            vmem_limit_bytes=_vmem_limit(48),
        ),
        name="outproj_residual",
    )(x, sb, df, w_out, w_out)


def _rope_tables(seq):
    pos = jnp.arange(seq, dtype=F32)
    inv_freq = ROPE_THETA ** (-jnp.arange(0, HEAD_DIM, 2, dtype=F32) / HEAD_DIM)
    ang = pos[:, None] * inv_freq[None, :]
    ang = jnp.concatenate([ang, ang], axis=-1)
    sign = jnp.concatenate([-jnp.ones((HEAD_DIM // 2,), F32), jnp.ones((HEAD_DIM // 2,), F32)])
    return jnp.cos(ang), jnp.sin(ang) * sign[None, :]


def kernel(x, norm_ffn1, w_ffn1_gate, w_ffn1_up, w_ffn1_down, norm_mix, w_in, lambda_q1, lambda_k1, lambda_q2, lambda_k2, diff_subln, w_out, norm_ffn2, w_ffn2_gate, w_ffn2_up, w_ffn2_down, norm_final):
    b, s, d = x.shape
    depth = w_in.shape[0]
    coef_a, coef_b = _inproj_tables(*_rope_tables(s))
    tq, tk = 1024, 256
    tri = (lax.broadcasted_iota(jnp.int32, (tk, tk), 0) > lax.broadcasted_iota(jnp.int32, (tk, tk), 1)).astype(BF16)
    g_final = norm_final.reshape(1, d)

    xt = x.reshape(b * s, d)
    for layer in range(depth):
        lambda_init = 0.8 - 0.6 * math.exp(-0.3 * layer)
        xt, (w_in16, w_out16) = _ffn(xt, norm_ffn1[layer].reshape(1, d), w_ffn1_gate, w_ffn1_up, w_ffn1_down,
                                     g_final, layer, final_norm=False, side=(w_in, w_out))
        proj = _inproj(xt, norm_mix[layer].reshape(1, d), w_in16, coef_a, coef_b, s)
        proj = proj.reshape(b, s, -1)
        sb, df = _attention(proj, tri, lambda_q1[layer].reshape(1, -1), lambda_k1[layer].reshape(1, -1),
                            lambda_q2[layer].reshape(1, -1), lambda_k2[layer].reshape(1, -1),
                            diff_subln[layer].reshape(1, -1), lambda_init, tq=tq, tk=tk)
        xt = _outproj(xt, sb.reshape(b * s, -1), df.reshape(b * s, -1), w_out16)
        xt, _ = _ffn(xt, norm_ffn2[layer].reshape(1, d), w_ffn2_gate, w_ffn2_up, w_ffn2_down, g_final, layer,
                     final_norm=(layer == depth - 1))
    return xt.reshape(b, s, d)
```

```python
import functools
import math

import jax
import jax.numpy as jnp
from jax import lax
from jax.experimental import pallas as pl
from jax.experimental.pallas import tpu as pltpu

HEAD_DIM = 128
N_SB_HEADS = 8
N_DIFF_HEADS = 4
SB_WIDTH = N_SB_HEADS * HEAD_DIM
DIFF_WIDTH = N_DIFF_HEADS * 2 * HEAD_DIM
ROPE_THETA = 10000.0
RMS_EPS = 1e-6
FFN_RESIDUAL_WEIGHT = 0.5
LOG2E = math.log2(math.e)
Q_SCALE = LOG2E / math.sqrt(HEAD_DIM)

SIDE_CAST_STEPS = 64
SOFTPLUS2_LINEAR = 64.0
INPROJ_ROW_CHUNKS = 4
DIFF_ROW_CHUNKS = 4
V7X_VMEM_BYTES = 64 * 1024 * 1024
MIB = 1024 * 1024
NEG_BIG = -1e30

F32 = jnp.float32
BF16 = jnp.bfloat16


def _vmem_limit(mib):
    limit = mib * MIB
    assert limit < V7X_VMEM_BYTES
    return limit


def _rmsnorm_rows(x, g):
    ms = jnp.mean(x * x, axis=-1, keepdims=True)
    return x * lax.rsqrt(ms + RMS_EPS) * g


def _dot(a, b):
    return jnp.dot(a, b, preferred_element_type=F32)


def _dot_nt(a, b):
    return lax.dot_general(a, b, (((1,), (1,)), ((), ())), preferred_element_type=F32)


def _ffn_kernel(x_ref, g_ref, wg_ref, wu_ref, wd_ref, gf_ref, *rest, final_norm, emit_bf16, n_side, side_steps):
    side_in = rest[:n_side]
    o_ref = rest[n_side]
    w16_refs = rest[n_side + 1:n_side + 4] if emit_bf16 else ()
    side_out = rest[len(rest) - 1 - n_side:len(rest) - 1]
    h_ref = rest[-1]
    i = pl.program_id(0)
    j = pl.program_id(1)
    nj = pl.num_programs(1)

    @pl.when(j == 0)
    def _():
        x = x_ref[...]
        h_ref[...] = _rmsnorm_rows(x, g_ref[...]).astype(BF16)
        o_ref[...] = x

    if n_side:
        @pl.when(i * nj + j < side_steps)
        def _():
            for src, dst in zip(side_in, side_out):
                dst[...] = src[...].astype(BF16)

    wg = wg_ref[...].astype(BF16)
    wu = wu_ref[...].astype(BF16)
    wd = wd_ref[...].astype(BF16)
    if emit_bf16:
        wg16_ref, wu16_ref, wd16_ref = w16_refs
        wg16_ref[...] = wg
        wu16_ref[...] = wu
        wd16_ref[...] = wd

    h = h_ref[...]
    gate = _dot(h, wg)
    up = _dot(h, wu)
    act = (gate / (1.0 + jnp.exp(-gate))) * (up * FFN_RESIDUAL_WEIGHT)
    o_ref[...] += _dot(act.astype(BF16), wd)

    if final_norm:
        @pl.when(j == nj - 1)
        def _():
            o_ref[...] = _rmsnorm_rows(o_ref[...], gf_ref[...])


def _ffn_call(x, g, wg, wu, wd, g_final, *, layer, final_norm, row_block0, n_row_blocks, tm, tf,
              side=(), side_layer=None):
    t, d = x.shape
    emit_bf16 = layer is not None
    d_ff = wg.shape[-1]
    nj = d_ff // tf
    assert d_ff % tf == 0 and (row_block0 + n_row_blocks) * tm <= t
    if emit_bf16:
        w_in_specs = [
            pl.BlockSpec((None, d, tf), lambda i, j: (layer, 0, j)),
            pl.BlockSpec((None, d, tf), lambda i, j: (layer, 0, j)),
            pl.BlockSpec((None, tf, d), lambda i, j: (layer, j, 0)),
        ]
    else:
        w_in_specs = [
            pl.BlockSpec((d, tf), lambda i, j: (0, j)),
            pl.BlockSpec((d, tf), lambda i, j: (0, j)),
            pl.BlockSpec((tf, d), lambda i, j: (j, 0)),
        ]
    out_shape = [jax.ShapeDtypeStruct((t, d), F32)]
    out_specs = [pl.BlockSpec((tm, d), lambda i, j: (i + row_block0, 0))]
    if emit_bf16:
        out_shape += [jax.ShapeDtypeStruct((d, d_ff), BF16), jax.ShapeDtypeStruct((d, d_ff), BF16),
                      jax.ShapeDtypeStruct((d_ff, d), BF16)]
        out_specs += [pl.BlockSpec((d, tf), lambda i, j: (0, j)), pl.BlockSpec((d, tf), lambda i, j: (0, j)),
                      pl.BlockSpec((tf, d), lambda i, j: (j, 0))]
    side_steps = SIDE_CAST_STEPS if side else 0
    assert side_steps <= n_row_blocks * nj
    side_in_specs = []
    for arr in side:
        _, rows, cols = arr.shape
        assert rows % side_steps == 0
        slab = rows // side_steps
        side_in_specs.append(pl.BlockSpec(
            (None, slab, cols), lambda i, j: (side_layer, jnp.minimum(i * nj + j, side_steps - 1), 0)))
        out_shape.append(jax.ShapeDtypeStruct((rows, cols), BF16))
        out_specs.append(pl.BlockSpec((slab, cols), lambda i, j: (jnp.minimum(i * nj + j, side_steps - 1), 0)))
    x_mode = dict(pipeline_mode=pl.Buffered(1)) if n_row_blocks == 1 else {}
    res = pl.pallas_call(
        functools.partial(_ffn_kernel, final_norm=final_norm, emit_bf16=emit_bf16, n_side=len(side),
                          side_steps=side_steps),
        out_shape=out_shape,
        grid=(n_row_blocks, nj),
        in_specs=[
            pl.BlockSpec((tm, d), lambda i, j: (i + row_block0, 0), **x_mode),
            pl.BlockSpec((1, d), lambda i, j: (0, 0)),
            *w_in_specs,
            pl.BlockSpec((1, d), lambda i, j: (0, 0)),
            *side_in_specs,
        ],
        out_specs=out_specs,
        scratch_shapes=[pltpu.VMEM((tm, d), BF16)],
        input_output_aliases={0: 0},
        compiler_params=pltpu.CompilerParams(
            dimension_semantics=("arbitrary", "arbitrary"),
            vmem_limit_bytes=_vmem_limit(58),
        ),
        name="ffn_swiglu_cast" if emit_bf16 else "ffn_swiglu",
    )(x, g, wg, wu, wd, g_final, *side)
    return res


def _ffn(x, g, wg32, wu32, wd32, g_final, layer, *, final_norm, side=(), tm=1024, tf=512, tf_first=256):
    t, _ = x.shape
    x, wg, wu, wd = _ffn_call(x, g, wg32, wu32, wd32, g_final, layer=layer, final_norm=final_norm,
                              row_block0=0, n_row_blocks=1, tm=tm, tf=tf_first)
    x, *side16 = _ffn_call(x, g, wg, wu, wd, g_final, layer=None, final_norm=final_norm,
                           row_block0=1, n_row_blocks=t // tm - 1, tm=tm, tf=tf, side=side, side_layer=layer)
    return x, side16


def _inproj_kernel(x_ref, g_ref, w_ref, *rest, tm, tn, groups):
    a_refs, b_refs = rest[:groups], rest[groups:2 * groups]
    o_ref, h_ref = rest[2 * groups:]
    j = pl.program_id(1)

    @pl.when(j == 0)
    def _():
        h_ref[...] = _rmsnorm_rows(x_ref[...], g_ref[...]).astype(BF16)

    rows = tm // INPROJ_ROW_CHUNKS
    chunks_per_group = tn // groups // HEAD_DIM
    for r in range(INPROJ_ROW_CHUNKS):
        rs = slice(r * rows, (r + 1) * rows)
        acc = _dot(h_ref[rs, :], w_ref[...])
        parts = []
        for c in range(tn // HEAD_DIM):
            yc = acc[:, c * HEAD_DIM:(c + 1) * HEAD_DIM]
            a = a_refs[c // chunks_per_group][rs, :]
            b = b_refs[c // chunks_per_group][rs, :]
            parts.append(yc * a + pltpu.roll(yc, HEAD_DIM // 2, 1) * b)
        o_ref[rs, :] = jnp.concatenate(parts, axis=1).astype(o_ref.dtype)


def _inproj_tables(cos, sin):
    one = jnp.ones_like(cos)
    zero = jnp.zeros_like(cos)
    return (jnp.stack([one, one * Q_SCALE, cos, cos * Q_SCALE]),
            jnp.stack([zero, zero, sin, sin * Q_SCALE]))


def _inproj(x, g, w, coef_a, coef_b, seq, *, tm=1024, tn=2048):
    t, d = x.shape
    d_in = w.shape[1]
    assert SB_WIDTH == DIFF_WIDTH and tn % SB_WIDTH == 0
    assert t % tm == 0 and d_in % tn == 0 and seq % tm == 0 and tm % INPROJ_ROW_CHUNKS == 0
    groups = tn // SB_WIDTH
    pos_blocks = seq // tm
    dq_start = 3 * SB_WIDTH
    dk_start = dq_start + DIFF_WIDTH
    dv_start = dk_start + DIFF_WIDTH

    def coef_spec(group):
        def index(i, j):
            col = j * tn + group * SB_WIDTH
            mode = jnp.where(col < SB_WIDTH, 1,
                             jnp.where(col < dq_start, 0, jnp.where(col < dk_start, 3, jnp.where(col < dv_start, 2, 0))))
            return (mode, i % pos_blocks, 0)
        return pl.BlockSpec((None, tm, HEAD_DIM), index)

    coef_specs = [coef_spec(gi) for gi in range(groups)]
    return pl.pallas_call(
        functools.partial(_inproj_kernel, tm=tm, tn=tn, groups=groups),
        out_shape=jax.ShapeDtypeStruct((t, d_in), BF16),
        grid=(t // tm, d_in // tn),
        in_specs=[
            pl.BlockSpec((tm, d), lambda i, j: (i, 0)),
            pl.BlockSpec((1, d), lambda i, j: (0, 0)),
            pl.BlockSpec((d, tn), lambda i, j: (0, j)),
            *coef_specs,
            *coef_specs,
        ],
        out_specs=pl.BlockSpec((tm, tn), lambda i, j: (i, j)),
        scratch_shapes=[pltpu.VMEM((tm, d), BF16)],
        compiler_params=pltpu.CompilerParams(
            dimension_semantics=("parallel", "arbitrary"),
            vmem_limit_bytes=_vmem_limit(54),
        ),
        name="inproj_rope",
    )(x, g, w, *([coef_a] * groups), *([coef_b] * groups))


def _mask_top_rows(x, mask, fill):
    n = mask.shape[0]
    top = jnp.where(mask, x[:n], fill)
    return top if n == x.shape[0] else jnp.concatenate([top, x[n:]], axis=0)


def _sb_tile(q, k, v, tri, carry, mask):
    z = _dot_nt(q, k)
    sp = jnp.where(z > SOFTPLUS2_LINEAR, z, jnp.log(1.0 + jnp.exp2(z)) * LOG2E)
    if mask is not None:
        sp = _mask_top_rows(sp, mask, 0.0)
    excl = _dot(sp.astype(BF16), tri)
    w = jnp.exp2((z - sp) - excl - carry)
    if mask is not None:
        w = _mask_top_rows(w, mask, 0.0)
    out = _dot(w.astype(BF16), v)
    return out, carry + (excl[:, 0:1] + sp[:, 0:1])


def _sb_phases(q, k, v, tri, acc_ref, carry_ref, i, *, tq, tk):
    nb = tq // tk

    def band():
        row = lax.broadcasted_iota(jnp.int32, (tk, tk), 0)
        col = lax.broadcasted_iota(jnp.int32, (tk, tk), 1)
        strict_lower = col < row
        acc = None
        carry = None
        for d in reversed(range(nb)):
            r0 = d * tk
            s0 = pl.multiple_of(i * tq + r0, tk)
            zero_c = jnp.zeros((tk, 1), F32)
            carry_in = zero_c if carry is None else jnp.concatenate([zero_c, carry], axis=0)
            out, carry = _sb_tile(q[r0:, :], k[pl.ds(s0, tk), :], v[pl.ds(s0, tk), :], tri, carry_in, strict_lower)
            acc = out if acc is None else out + jnp.concatenate([jnp.zeros((tk, HEAD_DIM), F32), acc], axis=0)
        acc_ref[...] = acc
        carry_ref[...] = carry

    def trip(step):
        qv = q[...]
        carry = carry_ref[...]
        total = None
        for u in range(nb):
            s0 = pl.multiple_of(i * tq - (step * nb + u + 1) * tk, tk)
            out, carry = _sb_tile(qv, k[pl.ds(s0, tk), :], v[pl.ds(s0, tk), :], tri, carry, None)
            total = out if total is None else total + out
            if u < nb - 1:
                yield
        acc_ref[...] += total
        carry_ref[...] = carry

    return band, trip


def _diff_phases(q, k, v, m_ref, l_ref, acc_ref, i, *, tq):
    def tile(s0, r0, r1, width, mask):
        zs = []
        for c in range(2):
            z = _dot_nt(q[r0:r1, c * HEAD_DIM:(c + 1) * HEAD_DIM], k[pl.ds(s0, width), c * HEAD_DIM:(c + 1) * HEAD_DIM])
            zs.append(z if mask is None else jnp.where(mask, z, NEG_BIG))
        vv = v[pl.ds(s0, width), :]
        for c in range(2):
            m_old = m_ref[c, r0:r1, :]
            m_new = jnp.maximum(m_old, jnp.max(zs[c], axis=-1, keepdims=True))
            alpha = jnp.exp2(m_old - m_new)
            p = jnp.exp2(zs[c] - m_new)
            l_ref[c, r0:r1, :] = alpha * l_ref[c, r0:r1, :] + jnp.sum(p, axis=-1, keepdims=True)
            acc_ref[c, r0:r1, :] = alpha * acc_ref[c, r0:r1, :] + _dot(p.astype(BF16), vv)
            m_ref[c, r0:r1, :] = m_new

    def band():
        m_ref[...] = jnp.full_like(m_ref, NEG_BIG)
        l_ref[...] = jnp.zeros_like(l_ref)
        acc_ref[...] = jnp.zeros_like(acc_ref)
        half = tq // 2
        row = lax.broadcasted_iota(jnp.int32, (half, half), 0)
        col = lax.broadcasted_iota(jnp.int32, (half, half), 1)
        lower = col <= row
        diag = pl.multiple_of(i * tq, tq)
        tile(diag, 0, half, half, lower)
        tile(diag, half, tq, tq, jnp.concatenate([jnp.ones((half, half), jnp.bool_), lower], axis=1))

    def trip(step):
        s0 = pl.multiple_of(i * tq - (step + 1) * tq, tq)
        rows = tq // DIFF_ROW_CHUNKS
        for c in range(DIFF_ROW_CHUNKS):
            tile(s0, c * rows, (c + 1) * rows, tq, None)
            if c < DIFF_ROW_CHUNKS - 1:
                yield

    return band, trip


def _diff_finish(acc_ref, l_ref, lq1_ref, lk1_ref, lq2_ref, lk2_ref, sub_ref, lambda_init):
    lam = (jnp.exp(jnp.sum(lq1_ref[...] * lk1_ref[...])) - jnp.exp(jnp.sum(lq2_ref[...] * lk2_ref[...]))
           + lambda_init)
    o = acc_ref[0] / l_ref[0] - lam * (acc_ref[1] / l_ref[1])
    return _rmsnorm_rows(o, sub_ref[...]) * (1.0 - lambda_init)


def _attn_kernel(sq_ref, sk_ref, sv_ref, tri_ref, dq_ref, dk_ref, dv_ref, lq1_ref, lk1_ref, lq2_ref, lk2_ref,
                 sub_ref, so_ref, do_ref, sacc_ref, scarry_ref, m_ref, l_ref, dacc_ref, *, tq, tk, sb_heads,
                 lambda_init):
    i = pl.program_id(2)
    tri = tri_ref[...]
    phases = []
    for e in range(sb_heads):
        cols = slice(e * HEAD_DIM, (e + 1) * HEAD_DIM)
        phases.append(_sb_phases(sq_ref.at[0, :, cols], sk_ref.at[0, :, cols], sv_ref.at[0, :, cols], tri,
                                 sacc_ref.at[e], scarry_ref.at[e], i, tq=tq, tk=tk))
    phases.append(_diff_phases(dq_ref.at[0], dk_ref.at[0], dv_ref.at[0], m_ref, l_ref, dacc_ref, i, tq=tq))

    for band, _ in phases:
        band()

    def body(step, _):
        live = [trip(step) for _, trip in phases]
        while live:
            live = [g for g in live if next(g, StopIteration) is not StopIteration]
        return 0

    lax.fori_loop(0, i, body, 0)

    for e in range(sb_heads):
        so_ref[0, :, e * HEAD_DIM:(e + 1) * HEAD_DIM] = sacc_ref[e].astype(so_ref.dtype)
    o = _diff_finish(dacc_ref, l_ref, lq1_ref, lk1_ref, lq2_ref, lk2_ref, sub_ref, lambda_init)
    do_ref[0] = o.astype(do_ref.dtype)


def _attention(proj, tri, lq1, lk1, lq2, lk2, subln, lambda_init, *, tq, tk):
    b, s, _ = proj.shape
    assert s % tq == 0 and tq % tk == 0 and tri.shape == (tk, tk) and N_SB_HEADS % N_DIFF_HEADS == 0
    sb_heads = N_SB_HEADS // N_DIFF_HEADS
    sw = sb_heads * HEAD_DIM
    dv = 2 * HEAD_DIM
    s_q, s_k, s_v = 0, SB_WIDTH // sw, 2 * SB_WIDTH // sw
    d_q = 3 * SB_WIDTH // dv
    d_k = d_q + N_DIFF_HEADS
    d_v = d_k + N_DIFF_HEADS
    vec = pl.BlockSpec((1, HEAD_DIM), lambda bi, h, i: (0, 0))
    return pl.pallas_call(
        functools.partial(_attn_kernel, tq=tq, tk=tk, sb_heads=sb_heads, lambda_init=lambda_init),
        out_shape=[jax.ShapeDtypeStruct((b, s, SB_WIDTH), BF16), jax.ShapeDtypeStruct((b, s, DIFF_WIDTH), BF16)],
        grid=(b, N_DIFF_HEADS, s // tq),
        in_specs=[
            pl.BlockSpec((1, tq, sw), lambda bi, h, i: (bi, i, s_q + h)),
            pl.BlockSpec((1, s, sw), lambda bi, h, i: (bi, 0, s_k + h)),
            pl.BlockSpec((1, s, sw), lambda bi, h, i: (bi, 0, s_v + h)),
            pl.BlockSpec((tk, tk), lambda bi, h, i: (0, 0)),
            pl.BlockSpec((1, tq, dv), lambda bi, h, i: (bi, i, d_q + h)),
            pl.BlockSpec((1, s, dv), lambda bi, h, i: (bi, 0, d_k + h)),
            pl.BlockSpec((1, s, dv), lambda bi, h, i: (bi, 0, d_v + h)),
            vec, vec, vec, vec,
            pl.BlockSpec((1, dv), lambda bi, h, i: (0, 0)),
        ],
        out_specs=[pl.BlockSpec((1, tq, sw), lambda bi, h, i: (bi, i, h)),
                   pl.BlockSpec((1, tq, dv), lambda bi, h, i: (bi, i, h))],
        scratch_shapes=[
            pltpu.VMEM((sb_heads, tq, HEAD_DIM), F32), pltpu.VMEM((sb_heads, tq, 1), F32),
            pltpu.VMEM((2, tq, 1), F32), pltpu.VMEM((2, tq, 1), F32), pltpu.VMEM((2, tq, dv), F32),
        ],
        compiler_params=pltpu.CompilerParams(
            dimension_semantics=("parallel", "parallel", "arbitrary"),
            vmem_limit_bytes=_vmem_limit(56),
        ),
        name="attn_sb_diff",
    )(proj, proj, proj, tri, proj, proj, proj, lq1, lk1, lq2, lk2, subln)


def _outproj_kernel(x_ref, sb_ref, df_ref, w1_ref, w2_ref, o_ref):
    o_ref[...] = x_ref[...] + _dot(sb_ref[...], w1_ref[...]) + _dot(df_ref[...], w2_ref[...])


def _outproj(x, sb, df, w_out, *, tm=512, tn=2048):
    t, d = x.shape
    assert t % tm == 0 and d % tn == 0
    k1 = sb.shape[1]
    k2 = df.shape[1]
    assert k1 % tn == 0 or tn % k1 == 0
    return pl.pallas_call(
        _outproj_kernel,
        out_shape=jax.ShapeDtypeStruct((t, d), F32),
        grid=(t // tm, d // tn),
        in_specs=[
            pl.BlockSpec((tm, tn), lambda i, j: (i, j)),
            pl.BlockSpec((tm, k1), lambda i, j: (i, 0)),
            pl.BlockSpec((tm, k2), lambda i, j: (i, 0)),
            pl.BlockSpec((k1, tn), lambda i, j: (0, j)),
            pl.BlockSpec((k2, tn), lambda i, j: (1, j)),
        ],
        out_specs=pl.BlockSpec((tm, tn), lambda i, j: (i, j)),
        compiler_params=pltpu.CompilerParams(
            dimension_semantics=("parallel", "arbitrary"),
            vmem_limit_bytes=_vmem_limit(48),
        ),
        name="outproj_residual",
    )(x, sb, df, w_out, w_out)


def _rope_tables(seq):
    pos = jnp.arange(seq, dtype=F32)
    inv_freq = ROPE_THETA ** (-jnp.arange(0, HEAD_DIM, 2, dtype=F32) / HEAD_DIM)
    ang = pos[:, None] * inv_freq[None, :]
    ang = jnp.concatenate([ang, ang], axis=-1)
    sign = jnp.concatenate([-jnp.ones((HEAD_DIM // 2,), F32), jnp.ones((HEAD_DIM // 2,), F32)])
    return jnp.cos(ang), jnp.sin(ang) * sign[None, :]


def kernel(x, norm_ffn1, w_ffn1_gate, w_ffn1_up, w_ffn1_down, norm_mix, w_in, lambda_q1, lambda_k1, lambda_q2, lambda_k2, diff_subln, w_out, norm_ffn2, w_ffn2_gate, w_ffn2_up, w_ffn2_down, norm_final):
    b, s, d = x.shape
    depth = w_in.shape[0]
    coef_a, coef_b = _inproj_tables(*_rope_tables(s))
    tq, tk = 1024, 256
    tri = (lax.broadcasted_iota(jnp.int32, (tk, tk), 0) > lax.broadcasted_iota(jnp.int32, (tk, tk), 1)).astype(BF16)
    g_final = norm_final.reshape(1, d)

    xt = x.reshape(b * s, d)
    for layer in range(depth):
        lambda_init = 0.8 - 0.6 * math.exp(-0.3 * layer)
        xt, (w_in16, w_out16) = _ffn(xt, norm_ffn1[layer].reshape(1, d), w_ffn1_gate, w_ffn1_up, w_ffn1_down,
                                     g_final, layer, final_norm=False, side=(w_in, w_out))
        proj = _inproj(xt, norm_mix[layer].reshape(1, d), w_in16, coef_a, coef_b, s)
        proj = proj.reshape(b, s, -1)
        sb, df = _attention(proj, tri, lambda_q1[layer].reshape(1, -1), lambda_k1[layer].reshape(1, -1),
                            lambda_q2[layer].reshape(1, -1), lambda_k2[layer].reshape(1, -1),
                            diff_subln[layer].reshape(1, -1), lambda_init, tq=tq, tk=tk)
        xt = _outproj(xt, sb.reshape(b * s, -1), df.reshape(b * s, -1), w_out16)
        xt, _ = _ffn(xt, norm_ffn2[layer].reshape(1, d), w_ffn2_gate, w_ffn2_up, w_ffn2_down, g_final, layer,
                     final_norm=(layer == depth - 1))
    return xt.reshape(b, s, d)
```

```python
import functools
import math

import jax
import jax.numpy as jnp
from jax import lax
from jax.experimental import pallas as pl
from jax.experimental.pallas import tpu as pltpu

HEAD_DIM = 128
N_SB_HEADS = 8
N_DIFF_HEADS = 4
SB_WIDTH = N_SB_HEADS * HEAD_DIM
DIFF_WIDTH = N_DIFF_HEADS * 2 * HEAD_DIM
ROPE_THETA = 10000.0
RMS_EPS = 1e-6
FFN_RESIDUAL_WEIGHT = 0.5
LOG2E = math.log2(math.e)
Q_SCALE = LOG2E / math.sqrt(HEAD_DIM)

SIDE_CAST_STEPS = 64
SOFTPLUS2_LINEAR = 64.0
WEIGHT_RING_SLOTS = 3
INPROJ_ROW_CHUNKS = 4
DIFF_ROW_CHUNKS = 4
V7X_VMEM_BYTES = 64 * 1024 * 1024
MIB = 1024 * 1024
NEG_BIG = -1e30

F32 = jnp.float32
BF16 = jnp.bfloat16


def _vmem_limit(mib):
    limit = mib * MIB
    assert limit < V7X_VMEM_BYTES
    return limit


def _rmsnorm_rows(x, g):
    ms = jnp.mean(x * x, axis=-1, keepdims=True)
    return x * lax.rsqrt(ms + RMS_EPS) * g


def _dot(a, b):
    return jnp.dot(a, b, preferred_element_type=F32)


def _dot_nt(a, b):
    return lax.dot_general(a, b, (((1,), (1,)), ((), ())), preferred_element_type=F32)


def _ffn_kernel(x_ref, g_ref, wg_ref, wu_ref, wd_ref, gf_ref, *rest, final_norm, emit_bf16, n_side, side_steps,
                layer=None, tf=None):
    i = pl.program_id(0)
    j = pl.program_id(1)
    nj = pl.num_programs(1)
    if emit_bf16:
        assert n_side == 0
        o_ref, wg16_ref, wu16_ref, wd16_ref, h_ref, gbuf, ubuf, dbuf, sems = rest
        side_in = side_out = ()

        def tile_copy(which, tile, slot):
            c0 = pl.multiple_of(tile * tf, tf)
            if which < 2:
                src = (wg_ref, wu_ref)[which].at[layer, :, pl.ds(c0, tf)]
            else:
                src = wd_ref.at[layer, pl.ds(c0, tf), :]
            return pltpu.make_async_copy(src, (gbuf, ubuf, dbuf)[which].at[slot], sems.at[which, slot])

        def start_tile(tile):
            for which in range(3):
                tile_copy(which, tile, tile % WEIGHT_RING_SLOTS).start()

        @pl.when(j == 0)
        def _():
            for t0 in range(WEIGHT_RING_SLOTS - 1):
                start_tile(t0)

        @pl.when(j + (WEIGHT_RING_SLOTS - 1) < nj)
        def _():
            start_tile(j + (WEIGHT_RING_SLOTS - 1))
    else:
        side_in = rest[:n_side]
        o_ref = rest[n_side]
        side_out = rest[len(rest) - 1 - n_side:len(rest) - 1]
        h_ref = rest[-1]

    @pl.when(j == 0)
    def _():
        x = x_ref[...]
        h_ref[...] = _rmsnorm_rows(x, g_ref[...]).astype(BF16)
        o_ref[...] = x

    if n_side:
        @pl.when(i * nj + j < side_steps)
        def _():
            for src, dst in zip(side_in, side_out):
                dst[...] = src[...].astype(BF16)

    if emit_bf16:
        slot = j % WEIGHT_RING_SLOTS
        for which in range(3):
            tile_copy(which, j, slot).wait()
        wg = gbuf[slot].astype(BF16)
        wu = ubuf[slot].astype(BF16)
        wd = dbuf[slot].astype(BF16)
        wg16_ref[...] = wg
        wu16_ref[...] = wu
        wd16_ref[...] = wd
    else:
        wg = wg_ref[...]
        wu = wu_ref[...]
        wd = wd_ref[...]

    h = h_ref[...]
    gate = _dot(h, wg)
    up = _dot(h, wu)
    act = (gate / (1.0 + jnp.exp(-gate))) * (up * FFN_RESIDUAL_WEIGHT)
    o_ref[...] += _dot(act.astype(BF16), wd)

    if final_norm:
        @pl.when(j == nj - 1)
        def _():
            o_ref[...] = _rmsnorm_rows(o_ref[...], gf_ref[...])


def _ffn_call(x, g, wg, wu, wd, g_final, *, layer, final_norm, row_block0, n_row_blocks, tm, tf,
              side=(), side_layer=None):
    t, d = x.shape
    emit_bf16 = layer is not None
    d_ff = wg.shape[-1]
    nj = d_ff // tf
    assert d_ff % tf == 0 and (row_block0 + n_row_blocks) * tm <= t
    scratch_shapes = [pltpu.VMEM((tm, d), BF16)]
    if emit_bf16:
        assert n_row_blocks == 1 and not side and nj >= WEIGHT_RING_SLOTS - 1
        w_in_specs = [pl.BlockSpec(memory_space=pl.ANY)] * 3
        scratch_shapes += [
            pltpu.VMEM((WEIGHT_RING_SLOTS, d, tf), F32), pltpu.VMEM((WEIGHT_RING_SLOTS, d, tf), F32),
            pltpu.VMEM((WEIGHT_RING_SLOTS, tf, d), F32), pltpu.SemaphoreType.DMA((3, WEIGHT_RING_SLOTS)),
        ]
    else:
        w_in_specs = [
            pl.BlockSpec((d, tf), lambda i, j: (0, j)),
            pl.BlockSpec((d, tf), lambda i, j: (0, j)),
            pl.BlockSpec((tf, d), lambda i, j: (j, 0)),
        ]
    out_shape = [jax.ShapeDtypeStruct((t, d), F32)]
    out_specs = [pl.BlockSpec((tm, d), lambda i, j: (i + row_block0, 0))]
    if emit_bf16:
        out_shape += [jax.ShapeDtypeStruct((d, d_ff), BF16), jax.ShapeDtypeStruct((d, d_ff), BF16),
                      jax.ShapeDtypeStruct((d_ff, d), BF16)]
        out_specs += [pl.BlockSpec((d, tf), lambda i, j: (0, j)), pl.BlockSpec((d, tf), lambda i, j: (0, j)),
                      pl.BlockSpec((tf, d), lambda i, j: (j, 0))]
    side_steps = SIDE_CAST_STEPS if side else 0
    assert side_steps <= n_row_blocks * nj
    side_in_specs = []
    for arr in side:
        _, rows, cols = arr.shape
        assert rows % side_steps == 0
        slab = rows // side_steps
        side_in_specs.append(pl.BlockSpec(
            (None, slab, cols), lambda i, j: (side_layer, jnp.minimum(i * nj + j, side_steps - 1), 0)))
        out_shape.append(jax.ShapeDtypeStruct((rows, cols), BF16))
        out_specs.append(pl.BlockSpec((slab, cols), lambda i, j: (jnp.minimum(i * nj + j, side_steps - 1), 0)))
    x_mode = dict(pipeline_mode=pl.Buffered(1)) if n_row_blocks == 1 else {}
    res = pl.pallas_call(
        functools.partial(_ffn_kernel, final_norm=final_norm, emit_bf16=emit_bf16, n_side=len(side),
                          side_steps=side_steps, layer=layer, tf=tf),
        out_shape=out_shape,
        grid=(n_row_blocks, nj),
        in_specs=[
            pl.BlockSpec((tm, d), lambda i, j: (i + row_block0, 0), **x_mode),
            pl.BlockSpec((1, d), lambda i, j: (0, 0)),
            *w_in_specs,
            pl.BlockSpec((1, d), lambda i, j: (0, 0)),
            *side_in_specs,
        ],
        out_specs=out_specs,
        scratch_shapes=scratch_shapes,
        input_output_aliases={0: 0},
        compiler_params=pltpu.CompilerParams(
            dimension_semantics=("arbitrary", "arbitrary"),
            vmem_limit_bytes=_vmem_limit(58),
        ),
        name="ffn_swiglu_cast" if emit_bf16 else "ffn_swiglu",
    )(x, g, wg, wu, wd, g_final, *side)
    return res


def _ffn(x, g, wg32, wu32, wd32, g_final, layer, *, final_norm, side=(), tm=1024, tf=512, tf_first=256):
    t, _ = x.shape
    x, wg, wu, wd = _ffn_call(x, g, wg32, wu32, wd32, g_final, layer=layer, final_norm=final_norm,
                              row_block0=0, n_row_blocks=1, tm=tm, tf=tf_first)
    x, *side16 = _ffn_call(x, g, wg, wu, wd, g_final, layer=None, final_norm=final_norm,
                           row_block0=1, n_row_blocks=t // tm - 1, tm=tm, tf=tf, side=side, side_layer=layer)
    return x, side16


def _inproj_kernel(x_ref, g_ref, w_ref, *rest, tm, tn, groups):
    a_refs, b_refs = rest[:groups], rest[groups:2 * groups]
    o_ref, h_ref = rest[2 * groups:]
    j = pl.program_id(1)

    @pl.when(j == 0)
    def _():
        h_ref[...] = _rmsnorm_rows(x_ref[...], g_ref[...]).astype(BF16)

    rows = tm // INPROJ_ROW_CHUNKS
    chunks_per_group = tn // groups // HEAD_DIM
    for r in range(INPROJ_ROW_CHUNKS):
        rs = slice(r * rows, (r + 1) * rows)
        acc = _dot(h_ref[rs, :], w_ref[...])
        parts = []
        for c in range(tn // HEAD_DIM):
            yc = acc[:, c * HEAD_DIM:(c + 1) * HEAD_DIM]
            a = a_refs[c // chunks_per_group][rs, :]
            b = b_refs[c // chunks_per_group][rs, :]
            parts.append(yc * a + pltpu.roll(yc, HEAD_DIM // 2, 1) * b)
        o_ref[rs, :] = jnp.concatenate(parts, axis=1).astype(o_ref.dtype)


def _inproj_tables(cos, sin):
    one = jnp.ones_like(cos)
    zero = jnp.zeros_like(cos)
    return (jnp.stack([one, one * Q_SCALE, cos, cos * Q_SCALE]),
            jnp.stack([zero, zero, sin, sin * Q_SCALE]))


def _inproj(x, g, w, coef_a, coef_b, seq, *, tm=1024, tn=2048):
    t, d = x.shape
    d_in = w.shape[1]
    assert SB_WIDTH == DIFF_WIDTH and tn % SB_WIDTH == 0
    assert t % tm == 0 and d_in % tn == 0 and seq % tm == 0 and tm % INPROJ_ROW_CHUNKS == 0
    groups = tn // SB_WIDTH
    pos_blocks = seq // tm
    dq_start = 3 * SB_WIDTH
    dk_start = dq_start + DIFF_WIDTH
    dv_start = dk_start + DIFF_WIDTH

    def coef_spec(group):
        def index(i, j):
            col = j * tn + group * SB_WIDTH
            mode = jnp.where(col < SB_WIDTH, 1,
                             jnp.where(col < dq_start, 0, jnp.where(col < dk_start, 3, jnp.where(col < dv_start, 2, 0))))
            return (mode, i % pos_blocks, 0)
        return pl.BlockSpec((None, tm, HEAD_DIM), index)

    coef_specs = [coef_spec(gi) for gi in range(groups)]
    return pl.pallas_call(
        functools.partial(_inproj_kernel, tm=tm, tn=tn, groups=groups),
        out_shape=jax.ShapeDtypeStruct((t, d_in), BF16),
        grid=(t // tm, d_in // tn),
        in_specs=[
            pl.BlockSpec((tm, d), lambda i, j: (i, 0)),
            pl.BlockSpec((1, d), lambda i, j: (0, 0)),
            pl.BlockSpec((d, tn), lambda i, j: (0, j)),
            *coef_specs,
            *coef_specs,
        ],
        out_specs=pl.BlockSpec((tm, tn), lambda i, j: (i, j)),
        scratch_shapes=[pltpu.VMEM((tm, d), BF16)],
        compiler_params=pltpu.CompilerParams(
            dimension_semantics=("parallel", "arbitrary"),
            vmem_limit_bytes=_vmem_limit(54),
        ),
        name="inproj_rope",
    )(x, g, w, *([coef_a] * groups), *([coef_b] * groups))


def _mask_top_rows(x, mask, fill):
    n = mask.shape[0]
    top = jnp.where(mask, x[:n], fill)
    return top if n == x.shape[0] else jnp.concatenate([top, x[n:]], axis=0)


def _sb_tile(q, k, v, tri, carry, mask):
    z = _dot_nt(q, k)
    sp = jnp.where(z > SOFTPLUS2_LINEAR, z, jnp.log(1.0 + jnp.exp2(z)) * LOG2E)
    if mask is not None:
        sp = _mask_top_rows(sp, mask, 0.0)
    excl = _dot(sp.astype(BF16), tri)
    w = jnp.exp2((z - sp) - excl - carry)
    if mask is not None:
        w = _mask_top_rows(w, mask, 0.0)
    out = _dot(w.astype(BF16), v)
    return out, carry + (excl[:, 0:1] + sp[:, 0:1])


def _sb_phases(q, k, v, tri, acc_ref, carry_ref, i, *, tq, tk):
    nb = tq // tk

    def band():
        row = lax.broadcasted_iota(jnp.int32, (tk, tk), 0)
        col = lax.broadcasted_iota(jnp.int32, (tk, tk), 1)
        strict_lower = col < row
        acc = None
        carry = None
        for d in reversed(range(nb)):
            r0 = d * tk
            s0 = pl.multiple_of(i * tq + r0, tk)
            zero_c = jnp.zeros((tk, 1), F32)
            carry_in = zero_c if carry is None else jnp.concatenate([zero_c, carry], axis=0)
            out, carry = _sb_tile(q[r0:, :], k[pl.ds(s0, tk), :], v[pl.ds(s0, tk), :], tri, carry_in, strict_lower)
            acc = out if acc is None else out + jnp.concatenate([jnp.zeros((tk, HEAD_DIM), F32), acc], axis=0)
        acc_ref[...] = acc
        carry_ref[...] = carry

    def trip(step):
        qv = q[...]
        carry = carry_ref[...]
        total = None
        for u in range(nb):
            s0 = pl.multiple_of(i * tq - (step * nb + u + 1) * tk, tk)
            out, carry = _sb_tile(qv, k[pl.ds(s0, tk), :], v[pl.ds(s0, tk), :], tri, carry, None)
            total = out if total is None else total + out
            if u < nb - 1:
                yield
        acc_ref[...] += total
        carry_ref[...] = carry

    return band, trip


def _diff_phases(q, k, v, m_ref, l_ref, acc_ref, i, *, tq):
    def tile(s0, r0, r1, width, mask):
        zs = []
        for c in range(2):
            z = _dot_nt(q[r0:r1, c * HEAD_DIM:(c + 1) * HEAD_DIM], k[pl.ds(s0, width), c * HEAD_DIM:(c + 1) * HEAD_DIM])
            zs.append(z if mask is None else jnp.where(mask, z, NEG_BIG))
        vv = v[pl.ds(s0, width), :]
        for c in range(2):
            m_old = m_ref[c, r0:r1, :]
            m_new = jnp.maximum(m_old, jnp.max(zs[c], axis=-1, keepdims=True))
            alpha = jnp.exp2(m_old - m_new)
            p = jnp.exp2(zs[c] - m_new)
            l_ref[c, r0:r1, :] = alpha * l_ref[c, r0:r1, :] + jnp.sum(p, axis=-1, keepdims=True)
            acc_ref[c, r0:r1, :] = alpha * acc_ref[c, r0:r1, :] + _dot(p.astype(BF16), vv)
            m_ref[c, r0:r1, :] = m_new

    def band():
        m_ref[...] = jnp.full_like(m_ref, NEG_BIG)
        l_ref[...] = jnp.zeros_like(l_ref)
        acc_ref[...] = jnp.zeros_like(acc_ref)
        half = tq // 2
        row = lax.broadcasted_iota(jnp.int32, (half, half), 0)
        col = lax.broadcasted_iota(jnp.int32, (half, half), 1)
        lower = col <= row
        diag = pl.multiple_of(i * tq, tq)
        tile(diag, 0, half, half, lower)
        tile(diag, half, tq, tq, jnp.concatenate([jnp.ones((half, half), jnp.bool_), lower], axis=1))

    def trip(step):
        s0 = pl.multiple_of(i * tq - (step + 1) * tq, tq)
        rows = tq // DIFF_ROW_CHUNKS
        for c in range(DIFF_ROW_CHUNKS):
            tile(s0, c * rows, (c + 1) * rows, tq, None)
            if c < DIFF_ROW_CHUNKS - 1:
                yield

    return band, trip


def _diff_finish(acc_ref, l_ref, lq1_ref, lk1_ref, lq2_ref, lk2_ref, sub_ref, lambda_init):
    lam = (jnp.exp(jnp.sum(lq1_ref[...] * lk1_ref[...])) - jnp.exp(jnp.sum(lq2_ref[...] * lk2_ref[...]))
           + lambda_init)
    o = acc_ref[0] / l_ref[0] - lam * (acc_ref[1] / l_ref[1])
    return _rmsnorm_rows(o, sub_ref[...]) * (1.0 - lambda_init)


def _attn_kernel(sq_ref, sk_ref, sv_ref, tri_ref, dq_ref, dk_ref, dv_ref, lq1_ref, lk1_ref, lq2_ref, lk2_ref,
                 sub_ref, so_ref, do_ref, sacc_ref, scarry_ref, m_ref, l_ref, dacc_ref, *, tq, tk, sb_heads,
                 lambda_init):
    i = pl.program_id(2)
    tri = tri_ref[...]
    phases = []
    for e in range(sb_heads):
        cols = slice(e * HEAD_DIM, (e + 1) * HEAD_DIM)
        phases.append(_sb_phases(sq_ref.at[0, :, cols], sk_ref.at[0, :, cols], sv_ref.at[0, :, cols], tri,
                                 sacc_ref.at[e], scarry_ref.at[e], i, tq=tq, tk=tk))
    phases.append(_diff_phases(dq_ref.at[0], dk_ref.at[0], dv_ref.at[0], m_ref, l_ref, dacc_ref, i, tq=tq))

    for band, _ in phases:
        band()

    def body(step, _):
        live = [trip(step) for _, trip in phases]
        while live:
            live = [g for g in live if next(g, StopIteration) is not StopIteration]
        return 0

    lax.fori_loop(0, i, body, 0)

    for e in range(sb_heads):
        so_ref[0, :, e * HEAD_DIM:(e + 1) * HEAD_DIM] = sacc_ref[e].astype(so_ref.dtype)
    o = _diff_finish(dacc_ref, l_ref, lq1_ref, lk1_ref, lq2_ref, lk2_ref, sub_ref, lambda_init)
    do_ref[0] = o.astype(do_ref.dtype)


def _attention(proj, tri, lq1, lk1, lq2, lk2, subln, lambda_init, *, tq, tk):
    b, s, _ = proj.shape
    assert s % tq == 0 and tq % tk == 0 and tri.shape == (tk, tk) and N_SB_HEADS % N_DIFF_HEADS == 0
    sb_heads = N_SB_HEADS // N_DIFF_HEADS
    sw = sb_heads * HEAD_DIM
    dv = 2 * HEAD_DIM
    s_q, s_k, s_v = 0, SB_WIDTH // sw, 2 * SB_WIDTH // sw
    d_q = 3 * SB_WIDTH // dv
    d_k = d_q + N_DIFF_HEADS
    d_v = d_k + N_DIFF_HEADS
    vec = pl.BlockSpec((1, HEAD_DIM), lambda bi, h, i: (0, 0))
    return pl.pallas_call(
        functools.partial(_attn_kernel, tq=tq, tk=tk, sb_heads=sb_heads, lambda_init=lambda_init),
        out_shape=[jax.ShapeDtypeStruct((b, s, SB_WIDTH), BF16), jax.ShapeDtypeStruct((b, s, DIFF_WIDTH), BF16)],
        grid=(b, N_DIFF_HEADS, s // tq),
        in_specs=[
            pl.BlockSpec((1, tq, sw), lambda bi, h, i: (bi, i, s_q + h)),
            pl.BlockSpec((1, s, sw), lambda bi, h, i: (bi, 0, s_k + h)),
            pl.BlockSpec((1, s, sw), lambda bi, h, i: (bi, 0, s_v + h)),
            pl.BlockSpec((tk, tk), lambda bi, h, i: (0, 0)),
            pl.BlockSpec((1, tq, dv), lambda bi, h, i: (bi, i, d_q + h)),
            pl.BlockSpec((1, s, dv), lambda bi, h, i: (bi, 0, d_k + h)),
            pl.BlockSpec((1, s, dv), lambda bi, h, i: (bi, 0, d_v + h)),
            vec, vec, vec, vec,
            pl.BlockSpec((1, dv), lambda bi, h, i: (0, 0)),
        ],
        out_specs=[pl.BlockSpec((1, tq, sw), lambda bi, h, i: (bi, i, h)),
                   pl.BlockSpec((1, tq, dv), lambda bi, h, i: (bi, i, h))],
        scratch_shapes=[
            pltpu.VMEM((sb_heads, tq, HEAD_DIM), F32), pltpu.VMEM((sb_heads, tq, 1), F32),
            pltpu.VMEM((2, tq, 1), F32), pltpu.VMEM((2, tq, 1), F32), pltpu.VMEM((2, tq, dv), F32),
        ],
        compiler_params=pltpu.CompilerParams(
            dimension_semantics=("parallel", "parallel", "arbitrary"),
            vmem_limit_bytes=_vmem_limit(56),
        ),
        name="attn_sb_diff",
    )(proj, proj, proj, tri, proj, proj, proj, lq1, lk1, lq2, lk2, subln)


def _outproj_kernel(x_ref, sb_ref, df_ref, w1_ref, w2_ref, o_ref):
    o_ref[...] = x_ref[...] + _dot(sb_ref[...], w1_ref[...]) + _dot(df_ref[...], w2_ref[...])


def _outproj(x, sb, df, w_out, *, tm=512, tn=2048):
    t, d = x.shape
    assert t % tm == 0 and d % tn == 0
    k1 = sb.shape[1]
    k2 = df.shape[1]
    assert k1 % tn == 0 or tn % k1 == 0
    return pl.pallas_call(
        _outproj_kernel,
        out_shape=jax.ShapeDtypeStruct((t, d), F32),
        grid=(t // tm, d // tn),
        in_specs=[
            pl.BlockSpec((tm, tn), lambda i, j: (i, j)),
            pl.BlockSpec((tm, k1), lambda i, j: (i, 0)),
            pl.BlockSpec((tm, k2), lambda i, j: (i, 0)),
            pl.BlockSpec((k1, tn), lambda i, j: (0, j)),
            pl.BlockSpec((k2, tn), lambda i, j: (1, j)),
        ],
        out_specs=pl.BlockSpec((tm, tn), lambda i, j: (i, j)),
        compiler_params=pltpu.CompilerParams(
            dimension_semantics=("parallel", "arbitrary"),
            vmem_limit_bytes=_vmem_limit(48),
        ),
        name="outproj_residual",
    )(x, sb, df, w_out, w_out)


def _rope_tables(seq):
    pos = jnp.arange(seq, dtype=F32)
    inv_freq = ROPE_THETA ** (-jnp.arange(0, HEAD_DIM, 2, dtype=F32) / HEAD_DIM)
    ang = pos[:, None] * inv_freq[None, :]
    ang = jnp.concatenate([ang, ang], axis=-1)
    sign = jnp.concatenate([-jnp.ones((HEAD_DIM // 2,), F32), jnp.ones((HEAD_DIM // 2,), F32)])
    return jnp.cos(ang), jnp.sin(ang) * sign[None, :]


def kernel(x, norm_ffn1, w_ffn1_gate, w_ffn1_up, w_ffn1_down, norm_mix, w_in, lambda_q1, lambda_k1, lambda_q2, lambda_k2, diff_subln, w_out, norm_ffn2, w_ffn2_gate, w_ffn2_up, w_ffn2_down, norm_final):
    b, s, d = x.shape
    depth = w_in.shape[0]
    coef_a, coef_b = _inproj_tables(*_rope_tables(s))
    tq, tk = 1024, 256
    tri = (lax.broadcasted_iota(jnp.int32, (tk, tk), 0) > lax.broadcasted_iota(jnp.int32, (tk, tk), 1)).astype(BF16)
    g_final = norm_final.reshape(1, d)

    xt = x.reshape(b * s, d)
    for layer in range(depth):
        lambda_init = 0.8 - 0.6 * math.exp(-0.3 * layer)
        xt, (w_in16, w_out16) = _ffn(xt, norm_ffn1[layer].reshape(1, d), w_ffn1_gate, w_ffn1_up, w_ffn1_down,
                                     g_final, layer, final_norm=False, side=(w_in, w_out))
        proj = _inproj(xt, norm_mix[layer].reshape(1, d), w_in16, coef_a, coef_b, s)
        proj = proj.reshape(b, s, -1)
        sb, df = _attention(proj, tri, lambda_q1[layer].reshape(1, -1), lambda_k1[layer].reshape(1, -1),
                            lambda_q2[layer].reshape(1, -1), lambda_k2[layer].reshape(1, -1),
                            diff_subln[layer].reshape(1, -1), lambda_init, tq=tq, tk=tk)
        xt = _outproj(xt, sb.reshape(b * s, -1), df.reshape(b * s, -1), w_out16)
        xt, _ = _ffn(xt, norm_ffn2[layer].reshape(1, d), w_ffn2_gate, w_ffn2_up, w_ffn2_down, g_final, layer,
                     final_norm=(layer == depth - 1))
    return xt.reshape(b, s, d)
```

```python
import functools
import math

import jax
import jax.numpy as jnp
from jax import lax
from jax.experimental import pallas as pl
from jax.experimental.pallas import tpu as pltpu

HEAD_DIM = 128
N_SB_HEADS = 8
N_DIFF_HEADS = 4
SB_WIDTH = N_SB_HEADS * HEAD_DIM
DIFF_WIDTH = N_DIFF_HEADS * 2 * HEAD_DIM
ROPE_THETA = 10000.0
RMS_EPS = 1e-6
FFN_RESIDUAL_WEIGHT = 0.5
LOG2E = math.log2(math.e)
Q_SCALE = LOG2E / math.sqrt(HEAD_DIM)

SIDE_CAST_STEPS = 64
SOFTPLUS2_LINEAR = 64.0
WEIGHT_RING_SLOTS = 3
INPROJ_ROW_CHUNKS = 4
DIFF_ROW_CHUNKS = 4
V7X_VMEM_BYTES = 64 * 1024 * 1024
MIB = 1024 * 1024
NEG_BIG = -1e30

F32 = jnp.float32
BF16 = jnp.bfloat16


def _vmem_limit(mib):
    limit = mib * MIB
    assert limit < V7X_VMEM_BYTES
    return limit


def _rmsnorm_rows(x, g):
    ms = jnp.mean(x * x, axis=-1, keepdims=True)
    return x * lax.rsqrt(ms + RMS_EPS) * g


def _dot(a, b):
    return jnp.dot(a, b, preferred_element_type=F32)


def _dot_nt(a, b):
    return lax.dot_general(a, b, (((1,), (1,)), ((), ())), preferred_element_type=F32)


def _ffn_kernel(x_ref, g_ref, wg_ref, wu_ref, wd_ref, gf_ref, *rest, final_norm, emit_bf16, n_side, side_steps,
                layer=None, tf=None):
    i = pl.program_id(0)
    j = pl.program_id(1)
    nj = pl.num_programs(1)
    if emit_bf16:
        assert n_side == 0
        o_ref, wg16_ref, wu16_ref, wd16_ref, h_ref, gbuf, ubuf, dbuf, sems = rest
        side_in = side_out = ()

        def tile_copy(which, tile, slot):
            c0 = pl.multiple_of(tile * tf, tf)
            if which < 2:
                src = (wg_ref, wu_ref)[which].at[layer, :, pl.ds(c0, tf)]
            else:
                src = wd_ref.at[layer, pl.ds(c0, tf), :]
            return pltpu.make_async_copy(src, (gbuf, ubuf, dbuf)[which].at[slot], sems.at[which, slot])

        def start_tile(tile):
            for which in range(3):
                tile_copy(which, tile, tile % WEIGHT_RING_SLOTS).start(priority=which % 2)

        @pl.when(j == 0)
        def _():
            for t0 in range(WEIGHT_RING_SLOTS - 1):
                start_tile(t0)

        @pl.when(j + (WEIGHT_RING_SLOTS - 1) < nj)
        def _():
            start_tile(j + (WEIGHT_RING_SLOTS - 1))
    else:
        side_in = rest[:n_side]
        o_ref = rest[n_side]
        side_out = rest[len(rest) - 1 - n_side:len(rest) - 1]
        h_ref = rest[-1]

    @pl.when(j == 0)
    def _():
        x = x_ref[...]
        h_ref[...] = _rmsnorm_rows(x, g_ref[...]).astype(BF16)
        o_ref[...] = x

    if n_side:
        @pl.when(i * nj + j < side_steps)
        def _():
            for src, dst in zip(side_in, side_out):
                dst[...] = src[...].astype(BF16)

    if emit_bf16:
        slot = j % WEIGHT_RING_SLOTS
        for which in range(3):
            tile_copy(which, j, slot).wait()
        wg = gbuf[slot].astype(BF16)
        wu = ubuf[slot].astype(BF16)
        wd = dbuf[slot].astype(BF16)
        wg16_ref[...] = wg
        wu16_ref[...] = wu
        wd16_ref[...] = wd
    else:
        wg = wg_ref[...]
        wu = wu_ref[...]
        wd = wd_ref[...]

    h = h_ref[...]
    gate = _dot(h, wg)
    up = _dot(h, wu)
    act = (gate / (1.0 + jnp.exp(-gate))) * (up * FFN_RESIDUAL_WEIGHT)
    o_ref[...] += _dot(act.astype(BF16), wd)

    if final_norm:
        @pl.when(j == nj - 1)
        def _():
            o_ref[...] = _rmsnorm_rows(o_ref[...], gf_ref[...])


def _ffn_call(x, g, wg, wu, wd, g_final, *, layer, final_norm, row_block0, n_row_blocks, tm, tf,
              side=(), side_layer=None):
    t, d = x.shape
    emit_bf16 = layer is not None
    d_ff = wg.shape[-1]
    nj = d_ff // tf
    assert d_ff % tf == 0 and (row_block0 + n_row_blocks) * tm <= t
    scratch_shapes = [pltpu.VMEM((tm, d), BF16)]
    if emit_bf16:
        assert n_row_blocks == 1 and not side and nj >= WEIGHT_RING_SLOTS - 1
        w_in_specs = [pl.BlockSpec(memory_space=pl.ANY)] * 3
        scratch_shapes += [
            pltpu.VMEM((WEIGHT_RING_SLOTS, d, tf), F32), pltpu.VMEM((WEIGHT_RING_SLOTS, d, tf), F32),
            pltpu.VMEM((WEIGHT_RING_SLOTS, tf, d), F32), pltpu.SemaphoreType.DMA((3, WEIGHT_RING_SLOTS)),
        ]
    else:
        w_in_specs = [
            pl.BlockSpec((d, tf), lambda i, j: (0, j)),
            pl.BlockSpec((d, tf), lambda i, j: (0, j)),
            pl.BlockSpec((tf, d), lambda i, j: (j, 0)),
        ]
    out_shape = [jax.ShapeDtypeStruct((t, d), F32)]
    out_specs = [pl.BlockSpec((tm, d), lambda i, j: (i + row_block0, 0))]
    if emit_bf16:
        out_shape += [jax.ShapeDtypeStruct((d, d_ff), BF16), jax.ShapeDtypeStruct((d, d_ff), BF16),
                      jax.ShapeDtypeStruct((d_ff, d), BF16)]
        out_specs += [pl.BlockSpec((d, tf), lambda i, j: (0, j)), pl.BlockSpec((d, tf), lambda i, j: (0, j)),
                      pl.BlockSpec((tf, d), lambda i, j: (j, 0))]
    side_steps = SIDE_CAST_STEPS if side else 0
    assert side_steps <= n_row_blocks * nj
    side_in_specs = []
    for arr in side:
        _, rows, cols = arr.shape
        assert rows % side_steps == 0
        slab = rows // side_steps
        side_in_specs.append(pl.BlockSpec(
            (None, slab, cols), lambda i, j: (side_layer, jnp.minimum(i * nj + j, side_steps - 1), 0)))
        out_shape.append(jax.ShapeDtypeStruct((rows, cols), BF16))
        out_specs.append(pl.BlockSpec((slab, cols), lambda i, j: (jnp.minimum(i * nj + j, side_steps - 1), 0)))
    x_mode = dict(pipeline_mode=pl.Buffered(1)) if n_row_blocks == 1 else {}
    res = pl.pallas_call(
        functools.partial(_ffn_kernel, final_norm=final_norm, emit_bf16=emit_bf16, n_side=len(side),
                          side_steps=side_steps, layer=layer, tf=tf),
        out_shape=out_shape,
        grid=(n_row_blocks, nj),
        in_specs=[
            pl.BlockSpec((tm, d), lambda i, j: (i + row_block0, 0), **x_mode),
            pl.BlockSpec((1, d), lambda i, j: (0, 0)),
            *w_in_specs,
            pl.BlockSpec((1, d), lambda i, j: (0, 0)),
            *side_in_specs,
        ],
        out_specs=out_specs,
        scratch_shapes=scratch_shapes,
        input_output_aliases={0: 0},
        compiler_params=pltpu.CompilerParams(
            dimension_semantics=("arbitrary", "arbitrary"),
            vmem_limit_bytes=_vmem_limit(58),
        ),
        name="ffn_swiglu_cast" if emit_bf16 else "ffn_swiglu",
    )(x, g, wg, wu, wd, g_final, *side)
    return res


def _ffn(x, g, wg32, wu32, wd32, g_final, layer, *, final_norm, side=(), tm=1024, tf=512, tf_first=256):
    t, _ = x.shape
    x, wg, wu, wd = _ffn_call(x, g, wg32, wu32, wd32, g_final, layer=layer, final_norm=final_norm,
                              row_block0=0, n_row_blocks=1, tm=tm, tf=tf_first)
    x, *side16 = _ffn_call(x, g, wg, wu, wd, g_final, layer=None, final_norm=final_norm,
                           row_block0=1, n_row_blocks=t // tm - 1, tm=tm, tf=tf, side=side, side_layer=layer)
    return x, side16


def _inproj_kernel(x_ref, g_ref, w_ref, *rest, tm, tn, groups):
    a_refs, b_refs = rest[:groups], rest[groups:2 * groups]
    o_ref, h_ref = rest[2 * groups:]
    j = pl.program_id(1)

    @pl.when(j == 0)
    def _():
        h_ref[...] = _rmsnorm_rows(x_ref[...], g_ref[...]).astype(BF16)

    rows = tm // INPROJ_ROW_CHUNKS
    chunks_per_group = tn // groups // HEAD_DIM
    for r in range(INPROJ_ROW_CHUNKS):
        rs = slice(r * rows, (r + 1) * rows)
        acc = _dot(h_ref[rs, :], w_ref[...])
        parts = []
        for c in range(tn // HEAD_DIM):
            yc = acc[:, c * HEAD_DIM:(c + 1) * HEAD_DIM]
            a = a_refs[c // chunks_per_group][rs, :]
            b = b_refs[c // chunks_per_group][rs, :]
            parts.append(yc * a + pltpu.roll(yc, HEAD_DIM // 2, 1) * b)
        o_ref[rs, :] = jnp.concatenate(parts, axis=1).astype(o_ref.dtype)


def _inproj_tables(cos, sin):
    one = jnp.ones_like(cos)
    zero = jnp.zeros_like(cos)
    return (jnp.stack([one, one * Q_SCALE, cos, cos * Q_SCALE]),
            jnp.stack([zero, zero, sin, sin * Q_SCALE]))


def _inproj(x, g, w, coef_a, coef_b, seq, *, tm=1024, tn=2048):
    t, d = x.shape
    d_in = w.shape[1]
    assert SB_WIDTH == DIFF_WIDTH and tn % SB_WIDTH == 0
    assert t % tm == 0 and d_in % tn == 0 and seq % tm == 0 and tm % INPROJ_ROW_CHUNKS == 0
    groups = tn // SB_WIDTH
    pos_blocks = seq // tm
    dq_start = 3 * SB_WIDTH
    dk_start = dq_start + DIFF_WIDTH
    dv_start = dk_start + DIFF_WIDTH

    def coef_spec(group):
        def index(i, j):
            col = j * tn + group * SB_WIDTH
            mode = jnp.where(col < SB_WIDTH, 1,
                             jnp.where(col < dq_start, 0, jnp.where(col < dk_start, 3, jnp.where(col < dv_start, 2, 0))))
            return (mode, i % pos_blocks, 0)
        return pl.BlockSpec((None, tm, HEAD_DIM), index)

    coef_specs = [coef_spec(gi) for gi in range(groups)]
    return pl.pallas_call(
        functools.partial(_inproj_kernel, tm=tm, tn=tn, groups=groups),
        out_shape=jax.ShapeDtypeStruct((t, d_in), BF16),
        grid=(t // tm, d_in // tn),
        in_specs=[
            pl.BlockSpec((tm, d), lambda i, j: (i, 0)),
            pl.BlockSpec((1, d), lambda i, j: (0, 0)),
            pl.BlockSpec((d, tn), lambda i, j: (0, j)),
            *coef_specs,
            *coef_specs,
        ],
        out_specs=pl.BlockSpec((tm, tn), lambda i, j: (i, j)),
        scratch_shapes=[pltpu.VMEM((tm, d), BF16)],
        compiler_params=pltpu.CompilerParams(
            dimension_semantics=("parallel", "arbitrary"),
            vmem_limit_bytes=_vmem_limit(54),
        ),
        name="inproj_rope",
    )(x, g, w, *([coef_a] * groups), *([coef_b] * groups))


def _mask_top_rows(x, mask, fill):
    n = mask.shape[0]
    top = jnp.where(mask, x[:n], fill)
    return top if n == x.shape[0] else jnp.concatenate([top, x[n:]], axis=0)


def _sb_tile(q, k, v, tri, carry, mask):
    z = _dot_nt(q, k)
    sp = jnp.where(z > SOFTPLUS2_LINEAR, z, jnp.log(1.0 + jnp.exp2(z)) * LOG2E)
    if mask is not None:
        sp = _mask_top_rows(sp, mask, 0.0)
    excl = _dot(sp.astype(BF16), tri)
    w = jnp.exp2((z - sp) - excl - carry)
    if mask is not None:
        w = _mask_top_rows(w, mask, 0.0)
    out = _dot(w.astype(BF16), v)
    return out, carry + (excl[:, 0:1] + sp[:, 0:1])


def _sb_phases(q, k, v, tri, acc_ref, carry_ref, i, *, tq, tk):
    nb = tq // tk

    def band():
        row = lax.broadcasted_iota(jnp.int32, (tk, tk), 0)
        col = lax.broadcasted_iota(jnp.int32, (tk, tk), 1)
        strict_lower = col < row
        acc = None
        carry = None
        for d in reversed(range(nb)):
            r0 = d * tk
            s0 = pl.multiple_of(i * tq + r0, tk)
            zero_c = jnp.zeros((tk, 1), F32)
            carry_in = zero_c if carry is None else jnp.concatenate([zero_c, carry], axis=0)
            out, carry = _sb_tile(q[r0:, :], k[pl.ds(s0, tk), :], v[pl.ds(s0, tk), :], tri, carry_in, strict_lower)
            acc = out if acc is None else out + jnp.concatenate([jnp.zeros((tk, HEAD_DIM), F32), acc], axis=0)
        acc_ref[...] = acc
        carry_ref[...] = carry

    def trip(step):
        qv = q[...]
        carry = carry_ref[...]
        total = None
        for u in range(nb):
            s0 = pl.multiple_of(i * tq - (step * nb + u + 1) * tk, tk)
            out, carry = _sb_tile(qv, k[pl.ds(s0, tk), :], v[pl.ds(s0, tk), :], tri, carry, None)
            total = out if total is None else total + out
            if u < nb - 1:
                yield
        acc_ref[...] += total
        carry_ref[...] = carry

    return band, trip


def _diff_phases(q, k, v, m_ref, l_ref, acc_ref, i, *, tq):
    def tile(s0, r0, r1, width, mask):
        zs = []
        for c in range(2):
            z = _dot_nt(q[r0:r1, c * HEAD_DIM:(c + 1) * HEAD_DIM], k[pl.ds(s0, width), c * HEAD_DIM:(c + 1) * HEAD_DIM])
            zs.append(z if mask is None else jnp.where(mask, z, NEG_BIG))
        vv = v[pl.ds(s0, width), :]
        for c in range(2):
            m_old = m_ref[c, r0:r1, :]
            m_new = jnp.maximum(m_old, jnp.max(zs[c], axis=-1, keepdims=True))
            alpha = jnp.exp2(m_old - m_new)
            p = jnp.exp2(zs[c] - m_new)
            l_ref[c, r0:r1, :] = alpha * l_ref[c, r0:r1, :] + jnp.sum(p, axis=-1, keepdims=True)
            acc_ref[c, r0:r1, :] = alpha * acc_ref[c, r0:r1, :] + _dot(p.astype(BF16), vv)
            m_ref[c, r0:r1, :] = m_new

    def band():
        m_ref[...] = jnp.full_like(m_ref, NEG_BIG)
        l_ref[...] = jnp.zeros_like(l_ref)
        acc_ref[...] = jnp.zeros_like(acc_ref)
        half = tq // 2
        row = lax.broadcasted_iota(jnp.int32, (half, half), 0)
        col = lax.broadcasted_iota(jnp.int32, (half, half), 1)
        lower = col <= row
        diag = pl.multiple_of(i * tq, tq)
        tile(diag, 0, half, half, lower)
        tile(diag, half, tq, tq, jnp.concatenate([jnp.ones((half, half), jnp.bool_), lower], axis=1))

    def trip(step):
        s0 = pl.multiple_of(i * tq - (step + 1) * tq, tq)
        rows = tq // DIFF_ROW_CHUNKS
        for c in range(DIFF_ROW_CHUNKS):
            tile(s0, c * rows, (c + 1) * rows, tq, None)
            if c < DIFF_ROW_CHUNKS - 1:
                yield

    return band, trip


def _diff_finish(acc_ref, l_ref, lq1_ref, lk1_ref, lq2_ref, lk2_ref, sub_ref, lambda_init):
    lam = (jnp.exp(jnp.sum(lq1_ref[...] * lk1_ref[...])) - jnp.exp(jnp.sum(lq2_ref[...] * lk2_ref[...]))
           + lambda_init)
    o = acc_ref[0] / l_ref[0] - lam * (acc_ref[1] / l_ref[1])
    return _rmsnorm_rows(o, sub_ref[...]) * (1.0 - lambda_init)


def _attn_kernel(sq_ref, sk_ref, sv_ref, tri_ref, dq_ref, dk_ref, dv_ref, lq1_ref, lk1_ref, lq2_ref, lk2_ref,
                 sub_ref, so_ref, do_ref, sacc_ref, scarry_ref, m_ref, l_ref, dacc_ref, *, tq, tk, sb_heads,
                 lambda_init):
    i = pl.program_id(2)
    tri = tri_ref[...]
    phases = []
    for e in range(sb_heads):
        cols = slice(e * HEAD_DIM, (e + 1) * HEAD_DIM)
        phases.append(_sb_phases(sq_ref.at[0, :, cols], sk_ref.at[0, :, cols], sv_ref.at[0, :, cols], tri,
                                 sacc_ref.at[e], scarry_ref.at[e], i, tq=tq, tk=tk))
    phases.append(_diff_phases(dq_ref.at[0], dk_ref.at[0], dv_ref.at[0], m_ref, l_ref, dacc_ref, i, tq=tq))

    for band, _ in phases:
        band()

    def body(step, _):
        live = [trip(step) for _, trip in phases]
        while live:
            live = [g for g in live if next(g, StopIteration) is not StopIteration]
        return 0

    lax.fori_loop(0, i, body, 0)

    for e in range(sb_heads):
        so_ref[0, :, e * HEAD_DIM:(e + 1) * HEAD_DIM] = sacc_ref[e].astype(so_ref.dtype)
    o = _diff_finish(dacc_ref, l_ref, lq1_ref, lk1_ref, lq2_ref, lk2_ref, sub_ref, lambda_init)
    do_ref[0] = o.astype(do_ref.dtype)


def _attention(proj, tri, lq1, lk1, lq2, lk2, subln, lambda_init, *, tq, tk):
    b, s, _ = proj.shape
    assert s % tq == 0 and tq % tk == 0 and tri.shape == (tk, tk) and N_SB_HEADS % N_DIFF_HEADS == 0
    sb_heads = N_SB_HEADS // N_DIFF_HEADS
    sw = sb_heads * HEAD_DIM
    dv = 2 * HEAD_DIM
    s_q, s_k, s_v = 0, SB_WIDTH // sw, 2 * SB_WIDTH // sw
    d_q = 3 * SB_WIDTH // dv
    d_k = d_q + N_DIFF_HEADS
    d_v = d_k + N_DIFF_HEADS
    vec = pl.BlockSpec((1, HEAD_DIM), lambda bi, h, i: (0, 0))
    return pl.pallas_call(
        functools.partial(_attn_kernel, tq=tq, tk=tk, sb_heads=sb_heads, lambda_init=lambda_init),
        out_shape=[jax.ShapeDtypeStruct((b, s, SB_WIDTH), BF16), jax.ShapeDtypeStruct((b, s, DIFF_WIDTH), BF16)],
        grid=(b, N_DIFF_HEADS, s // tq),
        in_specs=[
            pl.BlockSpec((1, tq, sw), lambda bi, h, i: (bi, i, s_q + h)),
            pl.BlockSpec((1, s, sw), lambda bi, h, i: (bi, 0, s_k + h)),
            pl.BlockSpec((1, s, sw), lambda bi, h, i: (bi, 0, s_v + h)),
            pl.BlockSpec((tk, tk), lambda bi, h, i: (0, 0)),
            pl.BlockSpec((1, tq, dv), lambda bi, h, i: (bi, i, d_q + h)),
            pl.BlockSpec((1, s, dv), lambda bi, h, i: (bi, 0, d_k + h)),
            pl.BlockSpec((1, s, dv), lambda bi, h, i: (bi, 0, d_v + h)),
            vec, vec, vec, vec,
            pl.BlockSpec((1, dv), lambda bi, h, i: (0, 0)),
        ],
        out_specs=[pl.BlockSpec((1, tq, sw), lambda bi, h, i: (bi, i, h)),
                   pl.BlockSpec((1, tq, dv), lambda bi, h, i: (bi, i, h))],
        scratch_shapes=[
            pltpu.VMEM((sb_heads, tq, HEAD_DIM), F32), pltpu.VMEM((sb_heads, tq, 1), F32),
            pltpu.VMEM((2, tq, 1), F32), pltpu.VMEM((2, tq, 1), F32), pltpu.VMEM((2, tq, dv), F32),
        ],
        compiler_params=pltpu.CompilerParams(
            dimension_semantics=("parallel", "parallel", "arbitrary"),
            vmem_limit_bytes=_vmem_limit(56),
        ),
        name="attn_sb_diff",
    )(proj, proj, proj, tri, proj, proj, proj, lq1, lk1, lq2, lk2, subln)


def _outproj_kernel(x_ref, sb_ref, df_ref, w1_ref, w2_ref, o_ref):
    o_ref[...] = x_ref[...] + _dot(sb_ref[...], w1_ref[...]) + _dot(df_ref[...], w2_ref[...])


def _outproj(x, sb, df, w_out, *, tm=512, tn=2048):
    t, d = x.shape
    assert t % tm == 0 and d % tn == 0
    k1 = sb.shape[1]
    k2 = df.shape[1]
    assert k1 % tn == 0 or tn % k1 == 0
    return pl.pallas_call(
        _outproj_kernel,
        out_shape=jax.ShapeDtypeStruct((t, d), F32),
        grid=(t // tm, d // tn),
        in_specs=[
            pl.BlockSpec((tm, tn), lambda i, j: (i, j)),
            pl.BlockSpec((tm, k1), lambda i, j: (i, 0)),
            pl.BlockSpec((tm, k2), lambda i, j: (i, 0)),
            pl.BlockSpec((k1, tn), lambda i, j: (0, j)),
            pl.BlockSpec((k2, tn), lambda i, j: (1, j)),
        ],
        out_specs=pl.BlockSpec((tm, tn), lambda i, j: (i, j)),
        compiler_params=pltpu.CompilerParams(
            dimension_semantics=("parallel", "arbitrary"),
            vmem_limit_bytes=_vmem_limit(48),
        ),
        name="outproj_residual",
    )(x, sb, df, w_out, w_out)


def _rope_tables(seq):
    pos = jnp.arange(seq, dtype=F32)
    inv_freq = ROPE_THETA ** (-jnp.arange(0, HEAD_DIM, 2, dtype=F32) / HEAD_DIM)
    ang = pos[:, None] * inv_freq[None, :]
    ang = jnp.concatenate([ang, ang], axis=-1)
    sign = jnp.concatenate([-jnp.ones((HEAD_DIM // 2,), F32), jnp.ones((HEAD_DIM // 2,), F32)])
    return jnp.cos(ang), jnp.sin(ang) * sign[None, :]


def kernel(x, norm_ffn1, w_ffn1_gate, w_ffn1_up, w_ffn1_down, norm_mix, w_in, lambda_q1, lambda_k1, lambda_q2, lambda_k2, diff_subln, w_out, norm_ffn2, w_ffn2_gate, w_ffn2_up, w_ffn2_down, norm_final):
    b, s, d = x.shape
    depth = w_in.shape[0]
    coef_a, coef_b = _inproj_tables(*_rope_tables(s))
    tq, tk = 1024, 256
    tri = (lax.broadcasted_iota(jnp.int32, (tk, tk), 0) > lax.broadcasted_iota(jnp.int32, (tk, tk), 1)).astype(BF16)
    g_final = norm_final.reshape(1, d)

    xt = x.reshape(b * s, d)
    for layer in range(depth):
        lambda_init = 0.8 - 0.6 * math.exp(-0.3 * layer)
        xt, (w_in16, w_out16) = _ffn(xt, norm_ffn1[layer].reshape(1, d), w_ffn1_gate, w_ffn1_up, w_ffn1_down,
                                     g_final, layer, final_norm=False, side=(w_in, w_out))
        proj = _inproj(xt, norm_mix[layer].reshape(1, d), w_in16, coef_a, coef_b, s)
        proj = proj.reshape(b, s, -1)
        sb, df = _attention(proj, tri, lambda_q1[layer].reshape(1, -1), lambda_k1[layer].reshape(1, -1),
                            lambda_q2[layer].reshape(1, -1), lambda_k2[layer].reshape(1, -1),
                            diff_subln[layer].reshape(1, -1), lambda_init, tq=tq, tk=tk)
        xt = _outproj(xt, sb.reshape(b * s, -1), df.reshape(b * s, -1), w_out16)
        xt, _ = _ffn(xt, norm_ffn2[layer].reshape(1, d), w_ffn2_gate, w_ffn2_up, w_ffn2_down, g_final, layer,
                     final_norm=(layer == depth - 1))
    return xt.reshape(b, s, d)
```
